```python
import jax, jax.numpy as jnp
from jax import lax
import numpy as np

D_MODEL = 2048
BATCH = 2
SEQ = 16384
DEPTH = 2

CTX_LEN = 256
GRID_W = 64
RET_HEADS = 8
RET_HEAD_DIM = 128
RET_W = RET_HEADS * RET_HEAD_DIM
CONF_W = 512
CONF_KERNEL = 31
SC_W = 512
SC_KERNEL = 3
MIX_W = RET_W + CONF_W + SC_W
D_FF = 4 * D_MODEL
CHUNK = 128
ROPE_BASE = 10000.0
EPS = 1e-6
IN_W = 4 * RET_W + 2 * CONF_W + 3 * SC_W
IN_SPLITS = (RET_W, 2 * RET_W, 3 * RET_W, 4 * RET_W,
             4 * RET_W + CONF_W, 4 * RET_W + 2 * CONF_W,
             4 * RET_W + 2 * CONF_W + SC_W, 4 * RET_W + 2 * CONF_W + 2 * SC_W)

kernel_name = 'hybrid_retention_conformer_shortconv_dit'


def rms_norm(x, g):
    x32 = x.astype(jnp.float32)
    y = x32 * lax.rsqrt(jnp.mean(jnp.square(x32), axis=-1, keepdims=True) + EPS)
    return y.astype(x.dtype) * g


def layer_norm(x, g, b):
    x32 = x.astype(jnp.float32)
    mu = jnp.mean(x32, axis=-1, keepdims=True)
    var = jnp.mean(jnp.square(x32 - mu), axis=-1, keepdims=True)
    return ((x32 - mu) * lax.rsqrt(var + EPS)).astype(x.dtype) * g + b


def ada_modulation(cond, w, b):
    m = jax.nn.silu(cond) @ w + b
    return [t[..., None, :] for t in jnp.split(m, 6, axis=-1)]


def modulate(h, shift, scale):
    return h * (1 + scale) + shift


def to_heads(t):
    B, L, _ = t.shape
    return t.reshape(B, L, RET_HEADS, RET_HEAD_DIM).transpose(0, 2, 1, 3)


def rope_1d(x, pos):
    half = x.shape[-1] // 2
    freqs = ROPE_BASE ** (-jnp.arange(half, dtype=jnp.float32) / half)
    ang = pos.astype(jnp.float32)[:, None] * freqs
    cos = jnp.cos(ang).astype(x.dtype)
    sin = jnp.sin(ang).astype(x.dtype)
    x1, x2 = x[..., :half], x[..., half:]
    return jnp.concatenate([x1 * cos - x2 * sin, x1 * sin + x2 * cos], axis=-1)


def rope_2d(x, rows, cols):
    half = x.shape[-1] // 2
    return jnp.concatenate([rope_1d(x[..., :half], rows), rope_1d(x[..., half:], cols)], axis=-1)


def chunk_retention(q, k, v, log_gamma, s0):
    B, H, L, d = q.shape
    n = L // CHUNK
    dt = q.dtype

    def blocks(t):
        return jnp.moveaxis(t.reshape(B, H, n, CHUNK, d), 2, 0)

    i = jnp.arange(CHUNK, dtype=jnp.float32)
    lg = log_gamma[:, None, None]
    diff = i[:, None] - i[None, :]
    intra = jnp.where(diff >= 0, jnp.exp(lg * jnp.maximum(diff, 0.0)), 0.0).astype(dt)
    q_dec = jnp.exp(lg * (i[:, None] + 1.0)).astype(dt)
    k_dec = jnp.exp(lg * (CHUNK - 1.0 - i[:, None])).astype(dt)
    blk_dec = jnp.exp(lg * CHUNK)

    def step(s, qkv):
        qb, kb, vb = qkv
        scores = jnp.einsum('bhid,bhjd->bhij', qb, kb) * intra
        o = (jnp.einsum('bhij,bhjd->bhid', scores, vb)
             + jnp.einsum('bhid,bhde->bhie', qb * q_dec, s.astype(dt)))
        s = s * blk_dec + jnp.einsum('bhjd,bhje->bhde', kb * k_dec, vb).astype(jnp.float32)
        return s, o

    _, o = lax.scan(step, s0, (blocks(q), blocks(k), blocks(v)))
    return jnp.moveaxis(o, 0, 2).reshape(B, H, L, d)


def bidirectional_retention(q, k, v, lg_f, lg_b, s_f, s_b):
    fwd = chunk_retention(q, k, v, lg_f, s_f)
    bwd = chunk_retention(jnp.flip(q, 2), jnp.flip(k, 2), jnp.flip(v, 2), lg_b, s_b)
    return fwd + jnp.flip(bwd, 2)


def context_scan_states(k, v, lg_f, lg_b):
    L = k.shape[2]
    pos = jnp.arange(L, dtype=jnp.float32)
    w_f = jnp.exp(lg_f[:, None] * (L - 1.0 - pos))[:, :, None]
    w_b = jnp.exp(lg_b[:, None] * pos)[:, :, None]
    k32 = k.astype(jnp.float32)
    v32 = v.astype(jnp.float32)
    s_f = jnp.einsum('bhld,bhle->bhde', k32 * w_f, v32)
    s_b = jnp.einsum('bhld,bhle->bhde', k32 * w_b, v32)
    return s_f, s_b


def head_group_norm(o, g):
    B, H, L, d = o.shape
    o32 = o.astype(jnp.float32)
    mu = jnp.mean(o32, axis=-1, keepdims=True)
    var = jnp.mean(jnp.square(o32 - mu), axis=-1, keepdims=True)
    y = ((o32 - mu) * lax.rsqrt(var + EPS)).astype(o.dtype)
    return y.transpose(0, 2, 1, 3).reshape(B, L, H * d) * g


def depthwise_conv(u, w):
    K = w.shape[0]
    return lax.conv_general_dilated(
        u, w[:, None, :].astype(u.dtype), window_strides=(1,),
        padding=[(K // 2, K // 2)], dimension_numbers=('NWC', 'WIO', 'NWC'),
        feature_group_count=u.shape[-1])


def token_mix(parts, ret_o, ret_norm_g, conf_dw_w, conf_dw_b, conf_ln_g, conf_ln_b, sc_dw_w, w_out):
    gate, conf_a, conf_b, sc_x, sc_b, sc_c = parts[3:]
    ret = head_group_norm(ret_o, ret_norm_g) * jax.nn.silu(gate)
    u = conf_a * jax.nn.sigmoid(conf_b)
    u = depthwise_conv(u, conf_dw_w) + conf_dw_b
    conf = jax.nn.silu(layer_norm(u, conf_ln_g, conf_ln_b))
    sc = sc_b * depthwise_conv(sc_c * sc_x, sc_dw_w)
    return jnp.concatenate([ret, conf, sc], axis=-1) @ w_out


def mlp_sublayer(x, g_pre, g_post, shift, scale, gate, w1, w2):
    h = modulate(rms_norm(x, g_pre), shift, scale)
    y = jnp.square(jax.nn.relu(h @ w1)) @ w2
    return x + gate * rms_norm(y, g_post)


def setup_inputs(seed: int = 0) -> dict:
    key = jax.random.key(seed)
    ks = jax.random.split(key, 24)
    f32 = jnp.float32
    nrm = lambda k, shape, s: jax.random.normal(k, shape, f32) * s
    g0 = 1.0 - 2.0 ** (-5.0 - jnp.arange(RET_HEADS, dtype=f32))
    decay_logit = jnp.log(g0) - jnp.log1p(-g0)
    return {
        'x': nrm(ks[0], (BATCH, SEQ, D_MODEL), 1.0),
        'c': nrm(ks[1], (BATCH, D_MODEL), 1.0),
        'ctx': nrm(ks[2], (BATCH, CTX_LEN, D_MODEL), 1.0),
        'c_ctx': nrm(ks[3], (D_MODEL,), 1.0),
        'w_ada': nrm(ks[4], (DEPTH, D_MODEL, 6 * D_MODEL), 0.5 * D_MODEL ** -0.5),
        'b_ada': nrm(ks[5], (DEPTH, 6 * D_MODEL), 0.02),
        'g_pre_mix': 1.0 + nrm(ks[6], (DEPTH, D_MODEL), 0.05),
        'g_post_mix': 1.0 + nrm(ks[7], (DEPTH, D_MODEL), 0.05),
        'g_pre_ffn': 1.0 + nrm(ks[8], (DEPTH, D_MODEL), 0.05),
        'g_post_ffn': 1.0 + nrm(ks[9], (DEPTH, D_MODEL), 0.05),
        'w_in': nrm(ks[10], (DEPTH, D_MODEL, IN_W), D_MODEL ** -0.5),
        'ret_decay_fwd': decay_logit + nrm(ks[11], (DEPTH, RET_HEADS), 0.1),
        'ret_decay_bwd': decay_logit + nrm(ks[12], (DEPTH, RET_HEADS), 0.1),
        'ret_norm_g': 1.0 + nrm(ks[13], (DEPTH, RET_W), 0.05),
        'conf_dw_w': nrm(ks[14], (DEPTH, CONF_KERNEL, CONF_W), CONF_KERNEL ** -0.5),
        'conf_dw_b': nrm(ks[15], (DEPTH, CONF_W), 0.02),
        'conf_ln_g': 1.0 + nrm(ks[16], (DEPTH, CONF_W), 0.05),
        'conf_ln_b': nrm(ks[17], (DEPTH, CONF_W), 0.02),
        'sc_dw_w': nrm(ks[18], (DEPTH, SC_KERNEL, SC_W), SC_KERNEL ** -0.5),
        'w_out': nrm(ks[19], (DEPTH, MIX_W, D_MODEL), MIX_W ** -0.5),
        'w_ffn1': nrm(ks[20], (DEPTH, D_MODEL, D_FF), D_MODEL ** -0.5),
        'w_ffn2': nrm(ks[21], (DEPTH, D_FF, D_MODEL), D_FF ** -0.5),
    }


def reference(x, c, ctx, c_ctx, w_ada, b_ada, g_pre_mix, g_post_mix, g_pre_ffn, g_post_ffn,
              w_in, ret_decay_fwd, ret_decay_bwd, ret_norm_g, conf_dw_w, conf_dw_b,
              conf_ln_g, conf_ln_b, sc_dw_w, w_out, w_ffn1, w_ffn2):
    T = x.shape[1]
    rows_n = T // GRID_W
    rows = jnp.repeat(jnp.arange(rows_n), GRID_W)
    cols = jnp.tile(jnp.arange(GRID_W), rows_n)
    k_scale = RET_HEAD_DIM ** -0.5
    xc = ctx
    for l in range(DEPTH):
        last = l == DEPTH - 1
        sh1, sc1, gt1, sh2, sc2, gt2 = ada_modulation(c, w_ada[l], b_ada[l])
        csh1, csc1, cgt1, csh2, csc2, cgt2 = ada_modulation(c_ctx, w_ada[l], b_ada[l])
        lg_f = jax.nn.log_sigmoid(ret_decay_fwd[l].astype(jnp.float32))
        lg_b = jax.nn.log_sigmoid(ret_decay_bwd[l].astype(jnp.float32))
        mix_params = (ret_norm_g[l], conf_dw_w[l], conf_dw_b[l], conf_ln_g[l], conf_ln_b[l],
                      sc_dw_w[l], w_out[l])

        hc = modulate(rms_norm(xc, g_pre_mix[l]), csh1, csc1)
        if last:
            kc, vc = jnp.split(hc @ w_in[l][:, RET_W:3 * RET_W], 2, axis=-1)
        else:
            parts_c = jnp.split(hc @ w_in[l], IN_SPLITS, axis=-1)
            kc, vc = parts_c[1], parts_c[2]
        kc_h = to_heads(kc) * k_scale
        vc_h = to_heads(vc)
        s_f, s_b = context_scan_states(kc_h, vc_h, lg_f, lg_b)

        h = modulate(rms_norm(x, g_pre_mix[l]), sh1, sc1)
        parts = jnp.split(h @ w_in[l], IN_SPLITS, axis=-1)
        q_h = rope_2d(to_heads(parts[0]), rows, cols)
        k_h = rope_2d(to_heads(parts[1]), rows, cols) * k_scale
        v_h = to_heads(parts[2])
        ret_o = bidirectional_retention(q_h, k_h, v_h, lg_f, lg_b, s_f, s_b)
        y = token_mix(parts, ret_o, *mix_params)
        x = x + gt1 * rms_norm(y, g_post_mix[l])
        x = mlp_sublayer(x, g_pre_ffn[l], g_post_ffn[l], sh2, sc2, gt2, w_ffn1[l], w_ffn2[l])

        if not last:
            zeros = jnp.zeros_like(s_f)
            qc_h = to_heads(parts_c[0])
            ret_c = bidirectional_retention(qc_h, kc_h, vc_h, lg_f, lg_b, zeros, zeros)
            yc = token_mix(parts_c, ret_c, *mix_params)
            xc = xc + cgt1 * rms_norm(yc, g_post_mix[l])
            xc = mlp_sublayer(xc, g_pre_ffn[l], g_post_ffn[l], csh2, csc2, cgt2, w_ffn1[l], w_ffn2[l])
    return x
```

```python
import functools

import jax
import jax.numpy as jnp
from jax import lax
from jax.experimental import pallas as pl
from jax.experimental.pallas import tpu as pltpu

F32 = jnp.float32
BF16 = jnp.bfloat16

GRID_W = 64
HEADS = 8
HEAD_DIM = 128
RET_W = HEADS * HEAD_DIM
CONF_W = 512
CONF_KERNEL = 31
SC_W = 512
SC_KERNEL = 3
QKV_W = 3 * RET_W
REST_W = RET_W + 2 * CONF_W + 3 * SC_W
ROPE_BASE = 10000.0
EPS = 1e-6

LANES = 128
HALO = 16
ROWS = 32
RET_CHUNK = 256
RET_HEADS_PER_STEP = 2
VMEM_LIMIT = 58 * 1024 * 1024


def _rms(x, g):
    return x * lax.rsqrt(jnp.mean(x * x, axis=-1, keepdims=True) + EPS) * g


def _sigmoid(x):
    return 1.0 / (1.0 + jnp.exp(-x))


def _params(*sem):
    return pltpu.CompilerParams(dimension_semantics=sem, vmem_limit_bytes=VMEM_LIMIT)


def _resident(shape):
    return pl.BlockSpec(shape, lambda *_: (0,) * len(shape), pipeline_mode=pl.Buffered(1))


def _ada_kernel(cond_ref, w_ref, b_ref, o_ref):
    s = cond_ref[...]
    s = s * _sigmoid(s)
    o_ref[0] = jnp.dot(s, w_ref[0], preferred_element_type=F32,
                       precision=lax.Precision.HIGHEST) + b_ref[0]


def _ada_modulation(cond, w_ada, b_ada):
    depth, d, n = w_ada.shape
    r = cond.shape[0]
    tn = 1024
    return pl.pallas_call(
        _ada_kernel,
        grid=(depth, n // tn),
        in_specs=[pl.BlockSpec((r, d), lambda l, j: (0, 0)),
                  pl.BlockSpec((1, d, tn), lambda l, j: (l, 0, j)),
                  pl.BlockSpec((1, 1, tn), lambda l, j: (l, 0, j))],
        out_specs=pl.BlockSpec((1, r, tn), lambda l, j: (l, 0, j)),
        out_shape=jax.ShapeDtypeStruct((depth, r, n), F32),
        compiler_params=_params("parallel", "parallel"),
        name="ada_modulation",
    )(cond, w_ada, b_ada.reshape(depth, 1, n))


def _inproj_kernel(x_ref, g_ref, sh_ref, sc_ref, tab_ref, w_ref, qkv_ref, rest_ref, hb_ref, *, tm):
    g = g_ref[...]
    shift = sh_ref[0]
    scale1 = 1.0 + sc_ref[0]

    def norm_rows(i, carry):
        r = pl.multiple_of(i * ROWS, ROWS)
        h = _rms(x_ref[0, pl.ds(r, ROWS), :], g) * scale1 + shift
        hb_ref[pl.ds(r, ROWS), :] = h.astype(BF16)
        return carry

    lax.fori_loop(0, tm // ROWS, norm_rows, 0)
    hb = hb_ref[...]

    lane = lax.broadcasted_iota(jnp.int32, (tm, LANES), 1)
    first_of_pair = (lane & 32) == 0
    nt = 512
    for n in range((QKV_W + REST_W) // nt):
        a = jnp.dot(hb, w_ref[:, n * nt:(n + 1) * nt], preferred_element_type=F32)
        if n * nt < QKV_W:
            for j in range(nt // LANES):
                head = (n * nt) // LANES + j
                blk = a[:, j * LANES:(j + 1) * LANES]
                if head < 2 * HEADS:
                    t = 0 if head < HEADS else 2
                    swapped = jnp.where(first_of_pair, pltpu.roll(blk, 96, 1), pltpu.roll(blk, 32, 1))
                    blk = blk * tab_ref[t] + swapped * tab_ref[t + 1]
                qkv_ref[0, head] = blk.astype(BF16)
        else:
            c0 = n * nt - QKV_W
            rest_ref[0, :, c0:c0 + nt] = a.astype(BF16)


def _inproj(x, g_pre, shift, scale, tables, w_in):
    b, l, d = x.shape
    tm = min(512, l)
    return pl.pallas_call(
        functools.partial(_inproj_kernel, tm=tm),
        grid=(b, l // tm),
        in_specs=[pl.BlockSpec((1, tm, d), lambda i, m: (i, m, 0)),
                  pl.BlockSpec((1, d), lambda i, m: (0, 0)),
                  pl.BlockSpec((1, 1, d), lambda i, m: (i, 0, 0)),
                  pl.BlockSpec((1, 1, d), lambda i, m: (i, 0, 0)),
                  pl.BlockSpec((4, tm, LANES), lambda i, m: (0, m, 0)),
                  _resident(w_in.shape)],
        out_specs=[pl.BlockSpec((1, 3 * HEADS, tm, HEAD_DIM), lambda i, m: (i, 0, m, 0)),
                   pl.BlockSpec((1, tm, REST_W), lambda i, m: (i, m, 0))],
        out_shape=[jax.ShapeDtypeStruct((b, 3 * HEADS, l, HEAD_DIM), BF16),
                   jax.ShapeDtypeStruct((b, l, REST_W), BF16)],
        scratch_shapes=[pltpu.VMEM((tm, d), BF16)],
        compiler_params=_params("parallel", "parallel"),
        name="in_projection",
    )(x, g_pre, shift, scale, tables, w_in)


def _rope_tables(l, rotate):
    k_scale = HEAD_DIM ** -0.5
    if not rotate:
        one = jnp.ones((l, HEAD_DIM), F32)
        zero = jnp.zeros((l, HEAD_DIM), F32)
        return jnp.stack([one, zero, one * k_scale, zero])
    t = jnp.arange(l)
    half = HEAD_DIM // 4
    freqs = ROPE_BASE ** (-jnp.arange(half, dtype=F32) / half)
    ang_r = (t // GRID_W).astype(F32)[:, None] * freqs
    ang_c = (t % GRID_W).astype(F32)[:, None] * freqs
    cos = jnp.concatenate([jnp.cos(ang_r), jnp.cos(ang_r), jnp.cos(ang_c), jnp.cos(ang_c)], axis=-1)
    sin = jnp.concatenate([-jnp.sin(ang_r), jnp.sin(ang_r), -jnp.sin(ang_c), jnp.sin(ang_c)], axis=-1)
    return jnp.stack([cos, sin, cos * k_scale, sin * k_scale])


def _retention_kernel(lg_ref, q_ref, k_ref, v_ref, f0_ref, g0_ref, y_ref, ffin_ref, gfin_ref,
                      f_ref, g_ref, gs_ref, mask_ref, dec_ref, bd_ref, *, hb, ns, cps):
    c = RET_CHUNK
    h0 = pl.program_id(1) * hb
    s = pl.program_id(2)

    @pl.when(s == 0)
    def _init():
        ii = lax.broadcasted_iota(jnp.int32, (c, c), 0)
        jj = lax.broadcasted_iota(jnp.int32, (c, c), 1)
        diff = (ii - jj).astype(F32)
        pos = lax.broadcasted_iota(jnp.int32, (c, HEAD_DIM), 0).astype(F32)
        for hh in range(hb):
            lgf = lg_ref[0, h0 + hh]
            lgb = lg_ref[1, h0 + hh]
            mask_ref[hh] = (jnp.where(diff >= 0, jnp.exp(lgf * jnp.maximum(diff, 0.0)), 0.0)
                            + jnp.where(diff <= 0, jnp.exp(lgb * jnp.maximum(-diff, 0.0)), 0.0))
            dec_ref[hh, 0] = jnp.exp(lgf * (pos + 1.0))
            dec_ref[hh, 1] = jnp.exp(lgb * (c - pos))
            dec_ref[hh, 2] = jnp.exp(lgf * (c - 1.0 - pos))
            dec_ref[hh, 3] = jnp.exp(lgb * pos)
            bd_ref[hh, 0] = jnp.exp(jnp.full((HEAD_DIM, HEAD_DIM), lgf * c, F32))
            bd_ref[hh, 1] = jnp.exp(jnp.full((HEAD_DIM, HEAD_DIM), lgb * c, F32))
            f_ref[hh] = f0_ref[0, hh]
            g_ref[hh] = g0_ref[0, hh]

    tn_dims = (((0,), (0,)), ((), ()))
    nt_dims = (((1,), (1,)), ((), ()))

    @pl.when(s < ns)
    def _backward():
        def body(t, carry):
            j = cps - 1 - t
            n = (ns - 1 - s) * cps + j
            r = pl.multiple_of(j * c, c)
            for hh in range(hb):
                gs_ref[n, hh] = g_ref[hh].astype(BF16)
                k = k_ref[0, hh, pl.ds(r, c), :]
                v = v_ref[0, hh, pl.ds(r, c), :]
                kb = (k.astype(F32) * dec_ref[hh, 3]).astype(BF16)
                upd = lax.dot_general(kb, v, tn_dims, preferred_element_type=F32)
                g_ref[hh] = g_ref[hh] * bd_ref[hh, 1] + upd
            return carry

        lax.fori_loop(0, cps, body, 0)

        @pl.when(s == ns - 1)
        def _():
            for hh in range(hb):
                gfin_ref[0, hh] = g_ref[hh]

    @pl.when(s >= ns)
    def _forward():
        def body(j, carry):
            n = (s - ns) * cps + j
            r = pl.multiple_of(j * c, c)
            for hh in range(hb):
                q = q_ref[0, hh, pl.ds(r, c), :]
                k = k_ref[0, hh, pl.ds(r, c), :]
                v = v_ref[0, hh, pl.ds(r, c), :]
                q32 = q.astype(F32)
                scores = lax.dot_general(q, k, nt_dims, preferred_element_type=F32)
                p = (scores * mask_ref[hh]).astype(BF16)
                qd = jnp.concatenate([(q32 * dec_ref[hh, 0]).astype(BF16),
                                      (q32 * dec_ref[hh, 1]).astype(BF16)], axis=1)
                st = jnp.concatenate([f_ref[hh].astype(BF16), gs_ref[n, hh]], axis=0)
                o = (jnp.dot(p, v, preferred_element_type=F32)
                     + jnp.dot(qd, st, preferred_element_type=F32))
                mu = jnp.mean(o, axis=-1, keepdims=True)
                oc = o - mu
                var = jnp.mean(oc * oc, axis=-1, keepdims=True)
                y_ref[0, hh, pl.ds(r, c), :] = (oc * lax.rsqrt(var + EPS)).astype(BF16)
                kf = (k.astype(F32) * dec_ref[hh, 2]).astype(BF16)
                upd = lax.dot_general(kf, v, tn_dims, preferred_element_type=F32)
                f_ref[hh] = f_ref[hh] * bd_ref[hh, 0] + upd
            return carry

        lax.fori_loop(0, cps, body, 0)

        @pl.when(s == 2 * ns - 1)
        def _():
            for hh in range(hb):
                ffin_ref[0, hh] = f_ref[hh]


def _retention(qkv, lg, f0, g0):
    b, _, l, d = qkv.shape
    c = RET_CHUNK
    hb = RET_HEADS_PER_STEP
    sc = min(2048, l)
    ns = l // sc
    cps = sc // c

    def kv_block(s):
        return jnp.where(s < ns, ns - 1 - s, s - ns)

    def q_block(s):
        return jnp.maximum(s - ns, 0)

    hblocks = HEADS // hb
    state_spec = pl.BlockSpec((1, hb, d, d), lambda i, h, s: (i, h, 0, 0))
    return pl.pallas_call(
        functools.partial(_retention_kernel, hb=hb, ns=ns, cps=cps),
        grid=(b, hblocks, 2 * ns),
        in_specs=[pl.BlockSpec(memory_space=pltpu.SMEM),
                  pl.BlockSpec((1, hb, sc, d), lambda i, h, s: (i, h, q_block(s), 0)),
                  pl.BlockSpec((1, hb, sc, d), lambda i, h, s: (i, hblocks + h, kv_block(s), 0)),
                  pl.BlockSpec((1, hb, sc, d), lambda i, h, s: (i, 2 * hblocks + h, kv_block(s), 0)),
                  state_spec, state_spec],
        out_specs=[pl.BlockSpec((1, hb, sc, d), lambda i, h, s: (i, h, q_block(s), 0)),
                   state_spec, state_spec],
        out_shape=[jax.ShapeDtypeStruct((b, HEADS, l, d), BF16),
                   jax.ShapeDtypeStruct((b, HEADS, d, d), F32),
                   jax.ShapeDtypeStruct((b, HEADS, d, d), F32)],
        scratch_shapes=[pltpu.VMEM((hb, d, d), F32),
                        pltpu.VMEM((hb, d, d), F32),
                        pltpu.VMEM((l // c, hb, d, d), BF16),
                        pltpu.VMEM((hb, c, c), F32),
                        pltpu.VMEM((hb, 4, c, d), F32),
                        pltpu.VMEM((hb, 2, d, d), F32)],
        compiler_params=_params("parallel", "parallel", "arbitrary"),
        name="retention",
    )(lg, qkv, qkv, qkv, f0, g0)


def _mix_kernel(y_ref, rest_ref, prev_ref, next_ref, x_ref, gret_ref, cw_ref, cb_ref, lng_ref,
                lnb_ref, sw_ref, wout_ref, gpost_ref, gate_ref, o_ref,
                mixed_ref, uext_ref, zext_ref, proj_ref, *, tm):
    m = pl.program_id(1)
    last = pl.num_programs(1) - 1
    c_a, c_b = RET_W, RET_W + CONF_W
    c_x, c_bb, c_c = RET_W + 2 * CONF_W, RET_W + 2 * CONF_W + SC_W, RET_W + 2 * CONF_W + 2 * SC_W

    def glu(blk):
        a = blk[:, c_a:c_a + CONF_W].astype(F32)
        bgate = blk[:, c_b:c_b + CONF_W].astype(F32)
        return a * _sigmoid(bgate)

    def sc_in(blk):
        return blk[:, c_c:c_c + SC_W].astype(F32) * blk[:, c_x:c_x + SC_W].astype(F32)

    pv = prev_ref[0]
    nx = next_ref[0]
    uext_ref[0:HALO, :] = jnp.where(m > 0, glu(pv), 0.0)
    zext_ref[0:HALO, :] = jnp.where(m > 0, sc_in(pv), 0.0)
    uext_ref[HALO + tm:2 * HALO + tm, :] = jnp.where(m < last, glu(nx), 0.0)
    zext_ref[HALO + tm:2 * HALO + tm, :] = jnp.where(m < last, sc_in(nx), 0.0)

    gret = gret_ref[...]

    def gate_rows(i, carry):
        r = pl.multiple_of(i * ROWS, ROWS)
        blk = rest_ref[0, pl.ds(r, ROWS), :]
        gt = blk[:, 0:RET_W].astype(F32)
        y = jnp.concatenate([y_ref[0, h, pl.ds(r, ROWS), :] for h in range(HEADS)], axis=1)
        ret = y.astype(F32) * gret * (gt * _sigmoid(gt))
        mixed_ref[pl.ds(r, ROWS), 0:RET_W] = ret.astype(BF16)
        uext_ref[pl.ds(HALO + r, ROWS), :] = glu(blk)
        zext_ref[pl.ds(HALO + r, ROWS), :] = sc_in(blk)
        return carry

    lax.fori_loop(0, tm // ROWS, gate_rows, 0)

    cbias = cb_ref[...]
    lng = lng_ref[...]
    lnb = lnb_ref[...]
    pad_c = CONF_KERNEL // 2
    pad_s = SC_KERNEL // 2
    for r in range(0, tm, ROWS):
        acc = jnp.zeros((ROWS, CONF_W), F32) + cbias
        for j in range(CONF_KERNEL):
            o0 = HALO + r - pad_c + j
            acc = acc + uext_ref[o0:o0 + ROWS, :] * cw_ref[j:j + 1, :]
        mu = jnp.mean(acc, axis=-1, keepdims=True)
        ac = acc - mu
        var = jnp.mean(ac * ac, axis=-1, keepdims=True)
        u = ac * lax.rsqrt(var + EPS) * lng + lnb
        mixed_ref[r:r + ROWS, RET_W:RET_W + CONF_W] = (u * _sigmoid(u)).astype(BF16)
        zacc = jnp.zeros((ROWS, SC_W), F32)
        for j in range(SC_KERNEL):
            o0 = HALO + r - pad_s + j
            zacc = zacc + zext_ref[o0:o0 + ROWS, :] * sw_ref[j:j + 1, :]
        scb = rest_ref[0, r:r + ROWS, c_bb:c_bb + SC_W].astype(F32)
        mixed_ref[r:r + ROWS, RET_W + CONF_W:RET_W + CONF_W + SC_W] = (scb * zacc).astype(BF16)

    proj_ref[...] = jnp.dot(mixed_ref[...], wout_ref[...], preferred_element_type=F32)

    gpost = gpost_ref[...]
    gate = gate_ref[0]

    def out_rows(i, carry):
        r = pl.multiple_of(i * ROWS, ROWS)
        o_ref[0, pl.ds(r, ROWS), :] = (x_ref[0, pl.ds(r, ROWS), :]
                                       + gate * _rms(proj_ref[pl.ds(r, ROWS), :], gpost))
        return carry

    lax.fori_loop(0, tm // ROWS, out_rows, 0)


def _token_mix(y_ret, rest, x, ret_norm_g, conf_dw_w, conf_dw_b, conf_ln_g, conf_ln_b, sc_dw_w,
               w_out, g_post, gate):
    b, l, d = x.shape
    tm = min(512, l)
    hpt = tm // HALO
    nhalo = l // HALO
    row = lambda a: a.reshape(1, -1)
    const = lambda shape: pl.BlockSpec(shape, lambda i, m: (0,) * len(shape))
    return pl.pallas_call(
        functools.partial(_mix_kernel, tm=tm),
        grid=(b, l // tm),
        in_specs=[pl.BlockSpec((1, HEADS, tm, HEAD_DIM), lambda i, m: (i, 0, m, 0)),
                  pl.BlockSpec((1, tm, REST_W), lambda i, m: (i, m, 0)),
                  pl.BlockSpec((1, HALO, REST_W), lambda i, m: (i, jnp.maximum(m * hpt - 1, 0), 0)),
                  pl.BlockSpec((1, HALO, REST_W),
                               lambda i, m: (i, jnp.minimum((m + 1) * hpt, nhalo - 1), 0)),
                  pl.BlockSpec((1, tm, d), lambda i, m: (i, m, 0)),
                  const((1, RET_W)),
                  const((CONF_KERNEL, CONF_W)),
                  const((1, CONF_W)), const((1, CONF_W)), const((1, CONF_W)),
                  const((SC_KERNEL, SC_W)),
                  _resident(w_out.shape),
                  const((1, d)),
                  pl.BlockSpec((1, 1, d), lambda i, m: (i, 0, 0))],
        out_specs=pl.BlockSpec((1, tm, d), lambda i, m: (i, m, 0)),
        out_shape=jax.ShapeDtypeStruct((b, l, d), F32),
        scratch_shapes=[pltpu.VMEM((tm, RET_W + CONF_W + SC_W), BF16),
                        pltpu.VMEM((tm + 2 * HALO, CONF_W), F32),
                        pltpu.VMEM((tm + 2 * HALO, SC_W), F32),
                        pltpu.VMEM((tm, d), F32)],
        compiler_params=_params("parallel", "parallel"),
        name="token_mix",
    )(y_ret, rest, rest, rest, x, row(ret_norm_g), conf_dw_w, row(conf_dw_b), row(conf_ln_g),
      row(conf_ln_b), sc_dw_w, w_out, row(g_post), gate)


def _mlp_kernel(x_ref, gpre_ref, sh_ref, sc_ref, w1_ref, w2_ref, gpost_ref, gate_ref, o_ref, hb_ref,
                *, tm):
    f = pl.program_id(2)

    @pl.when(f == 0)
    def _prologue():
        g = gpre_ref[...]
        shift = sh_ref[0]
        scale1 = 1.0 + sc_ref[0]

        def norm_rows(i, carry):
            r = pl.multiple_of(i * ROWS, ROWS)
            h = _rms(x_ref[0, pl.ds(r, ROWS), :], g) * scale1 + shift
            hb_ref[pl.ds(r, ROWS), :] = h.astype(BF16)
            o_ref[0, pl.ds(r, ROWS), :] = jnp.zeros((ROWS, o_ref.shape[2]), F32)
            return carry

        lax.fori_loop(0, tm // ROWS, norm_rows, 0)

    h1 = jnp.dot(hb_ref[...], w1_ref[...], preferred_element_type=F32)
    a = jnp.maximum(h1, 0.0)
    o_ref[0] += jnp.dot((a * a).astype(BF16), w2_ref[...], preferred_element_type=F32)

    @pl.when(f == pl.num_programs(2) - 1)
    def _epilogue():
        gpost = gpost_ref[...]
        gate = gate_ref[0]

        def out_rows(i, carry):
            r = pl.multiple_of(i * ROWS, ROWS)
            o_ref[0, pl.ds(r, ROWS), :] = (x_ref[0, pl.ds(r, ROWS), :]
                                           + gate * _rms(o_ref[0, pl.ds(r, ROWS), :], gpost))
            return carry

        lax.fori_loop(0, tm // ROWS, out_rows, 0)


def _mlp(x, g_pre, shift, scale, w1, w2, g_post, gate):
    b, l, d = x.shape
    dff = w1.shape[1]
    tm = min(1024, l)
    tf = 512
    row = lambda a: a.reshape(1, -1)
    mod = pl.BlockSpec((1, 1, d), lambda i, m, f: (i, 0, 0))
    return pl.pallas_call(
        functools.partial(_mlp_kernel, tm=tm),
        grid=(b, l // tm, dff // tf),
        in_specs=[pl.BlockSpec((1, tm, d), lambda i, m, f: (i, m, 0)),
                  pl.BlockSpec((1, d), lambda i, m, f: (0, 0)),
                  mod, mod,
                  pl.BlockSpec((d, tf), lambda i, m, f: (0, f)),
                  pl.BlockSpec((tf, d), lambda i, m, f: (f, 0)),
                  pl.BlockSpec((1, d), lambda i, m, f: (0, 0)),
                  mod],
        out_specs=pl.BlockSpec((1, tm, d), lambda i, m, f: (i, m, 0)),
        out_shape=jax.ShapeDtypeStruct((b, l, d), F32),
        scratch_shapes=[pltpu.VMEM((tm, d), BF16)],
        compiler_params=_params("parallel", "parallel", "arbitrary"),
        name="mlp",
    )(x, row(g_pre), shift, scale, w1, w2, row(g_post), gate)


def kernel(x, c, ctx, c_ctx, w_ada, b_ada, g_pre_mix, g_post_mix, g_pre_ffn, g_post_ffn, w_in, ret_decay_fwd, ret_decay_bwd, ret_norm_g, conf_dw_w, conf_dw_b, conf_ln_g, conf_ln_b, sc_dw_w, w_out, w_ffn1, w_ffn2):
    depth = w_in.shape[0]
    b, l, d = x.shape
    lc = ctx.shape[1]

    cond = jnp.zeros((8, d), F32).at[:b].set(c).at[b].set(c_ctx)
    mods = _ada_modulation(cond, w_ada, b_ada)
    tab_x = _rope_tables(l, True)
    tab_c = _rope_tables(lc, False)
    zero_state = jnp.zeros((b, HEADS, HEAD_DIM, HEAD_DIM), F32)

    xc = ctx
    for layer in range(depth):
        last = layer == depth - 1
        mod_x = [t[:, None, :] for t in jnp.split(mods[layer, :b], 6, axis=-1)]
        mod_c = [jnp.broadcast_to(t[:, None, :], (b, 1, d))
                 for t in jnp.split(mods[layer, b:b + 1], 6, axis=-1)]
        lg = jnp.stack([jax.nn.log_sigmoid(ret_decay_fwd[layer].astype(F32)),
                        jax.nn.log_sigmoid(ret_decay_bwd[layer].astype(F32))])
        w_in_l = w_in[layer].astype(BF16)
        w_out_l = w_out[layer].astype(BF16)
        w1_l = w_ffn1[layer].astype(BF16)
        w2_l = w_ffn2[layer].astype(BF16)
        g_pre = g_pre_mix[layer].reshape(1, d)
        mix_params = (ret_norm_g[layer], conf_dw_w[layer], conf_dw_b[layer], conf_ln_g[layer],
                      conf_ln_b[layer], sc_dw_w[layer], w_out_l, g_post_mix[layer])
        mlp_params = (w1_l, w2_l, g_post_ffn[layer])

        qkv_c, rest_c = _inproj(xc, g_pre, mod_c[0], mod_c[1], tab_c, w_in_l)
        y_c, s_f, s_b = _retention(qkv_c, lg, zero_state, zero_state)

        qkv, rest = _inproj(x, g_pre, mod_x[0], mod_x[1], tab_x, w_in_l)
        y, _, _ = _retention(qkv, lg, s_f, s_b)
        x = _token_mix(y, rest, x, *mix_params, mod_x[2])
        x = _mlp(x, g_pre_ffn[layer], mod_x[3], mod_x[4], *mlp_params, mod_x[5])

        if not last:
            xc = _token_mix(y_c, rest_c, xc, *mix_params, mod_c[2])
            xc = _mlp(xc, g_pre_ffn[layer], mod_c[3], mod_c[4], *mlp_params, mod_c[5])
    return x
```

```python
import functools

import jax
import jax.numpy as jnp
from jax import lax
from jax.experimental import pallas as pl
from jax.experimental.pallas import tpu as pltpu

F32 = jnp.float32
BF16 = jnp.bfloat16

GRID_W = 64
HEADS = 8
HEAD_DIM = 128
RET_W = HEADS * HEAD_DIM
CONF_W = 512
CONF_KERNEL = 31
SC_W = 512
SC_KERNEL = 3
QKV_W = 3 * RET_W
REST_W = RET_W + 2 * CONF_W + 3 * SC_W
ROPE_BASE = 10000.0
EPS = 1e-6

LANES = 128
SUBLANES = 8
HALO = 16
ROWS = 32
RET_CHUNK = 256
RET_HEADS_PER_STEP = 2
VMEM_LIMIT = 58 * 1024 * 1024


def _rms(x, g):
    return x * lax.rsqrt(jnp.mean(x * x, axis=-1, keepdims=True) + EPS) * g


def _row_loop(tm, body, unroll=2):
    def step(i, carry):
        body(pl.multiple_of(i * ROWS, ROWS))
        return carry

    lax.fori_loop(0, tm // ROWS, step, 0, unroll=unroll)


def _norm_modulate(x_ref, g_ref, sh_ref, sc_ref, hb_ref, tm):
    gain = g_ref[...] * (1.0 + sc_ref[0])
    shift = sh_ref[0]

    def rows(r):
        hb_ref[pl.ds(r, ROWS), :] = (_rms(x_ref[0, pl.ds(r, ROWS), :], gain) + shift).astype(BF16)

    _row_loop(tm, rows)


def _gated_residual(x_ref, y_ref, gpost_ref, gate_ref, o_ref, tm):
    gain = gate_ref[0] * gpost_ref[...]

    def rows(r):
        o_ref[0, pl.ds(r, ROWS), :] = x_ref[0, pl.ds(r, ROWS), :] + _rms(y_ref[pl.ds(r, ROWS), :], gain)

    _row_loop(tm, rows)


def _sigmoid(x):
    return 1.0 / (1.0 + jnp.exp(-x))


def _params(*sem):
    return pltpu.CompilerParams(dimension_semantics=sem, vmem_limit_bytes=VMEM_LIMIT)


def _resident(shape):
    return pl.BlockSpec(shape, lambda *_: (0,) * len(shape), pipeline_mode=pl.Buffered(1))


def _ada_kernel(cond_ref, w_ref, b_ref, o_ref):
    s = cond_ref[...]
    s = s * _sigmoid(s)
    o_ref[0] = jnp.dot(s, w_ref[0], preferred_element_type=F32,
                       precision=lax.Precision.HIGHEST) + b_ref[0]


def _ada_modulation(cond, w_ada, b_ada):
    depth, d, n = w_ada.shape
    r = cond.shape[0]
    tn = 1024
    return pl.pallas_call(
        _ada_kernel,
        grid=(depth, n // tn),
        in_specs=[pl.BlockSpec((r, d), lambda l, j: (0, 0)),
                  pl.BlockSpec((1, d, tn), lambda l, j: (l, 0, j)),
                  pl.BlockSpec((1, 1, tn), lambda l, j: (l, 0, j))],
        out_specs=pl.BlockSpec((1, r, tn), lambda l, j: (l, 0, j)),
        out_shape=jax.ShapeDtypeStruct((depth, r, n), F32),
        compiler_params=_params("parallel", "parallel"),
        name="ada_modulation",
    )(cond, w_ada, b_ada.reshape(depth, 1, n))


def _inproj_kernel(x_ref, g_ref, sh_ref, sc_ref, tab_ref, w_ref, qkv_ref, rest_ref, hb_ref, *, tm):
    _norm_modulate(x_ref, g_ref, sh_ref, sc_ref, hb_ref, tm)
    hb = hb_ref[...]

    lane = lax.broadcasted_iota(jnp.int32, (tm, LANES), 1)
    first_of_pair = (lane & 32) == 0
    nt = 512
    for n in range((QKV_W + REST_W) // nt):
        a = jnp.dot(hb, w_ref[:, n * nt:(n + 1) * nt], preferred_element_type=F32)
        if n * nt < QKV_W:
            for j in range(nt // LANES):
                head = (n * nt) // LANES + j
                blk = a[:, j * LANES:(j + 1) * LANES]
                if head < 2 * HEADS:
                    t = 0 if head < HEADS else 2
                    swapped = jnp.where(first_of_pair, pltpu.roll(blk, 96, 1), pltpu.roll(blk, 32, 1))
                    blk = blk * tab_ref[t] + swapped * tab_ref[t + 1]
                qkv_ref[0, head] = blk.astype(BF16)
        else:
            c0 = n * nt - QKV_W
            rest_ref[0, :, c0:c0 + nt] = a.astype(BF16)


def _inproj(x, g_pre, shift, scale, tables, w_in):
    b, l, d = x.shape
    tm = min(512, l)
    return pl.pallas_call(
        functools.partial(_inproj_kernel, tm=tm),
        grid=(b, l // tm),
        in_specs=[pl.BlockSpec((1, tm, d), lambda i, m: (i, m, 0)),
                  pl.BlockSpec((1, d), lambda i, m: (0, 0)),
                  pl.BlockSpec((1, 1, d), lambda i, m: (i, 0, 0)),
                  pl.BlockSpec((1, 1, d), lambda i, m: (i, 0, 0)),
                  pl.BlockSpec((4, tm, LANES), lambda i, m: (0, m, 0)),
                  _resident(w_in.shape)],
        out_specs=[pl.BlockSpec((1, 3 * HEADS, tm, HEAD_DIM), lambda i, m: (i, 0, m, 0)),
                   pl.BlockSpec((1, tm, REST_W), lambda i, m: (i, m, 0))],
        out_shape=[jax.ShapeDtypeStruct((b, 3 * HEADS, l, HEAD_DIM), BF16),
                   jax.ShapeDtypeStruct((b, l, REST_W), BF16)],
        scratch_shapes=[pltpu.VMEM((tm, d), BF16)],
        compiler_params=_params("parallel", "parallel"),
        name="in_projection",
    )(x, g_pre, shift, scale, tables, w_in)


def _rope_tables(l, rotate):
    k_scale = HEAD_DIM ** -0.5
    if not rotate:
        one = jnp.ones((l, HEAD_DIM), F32)
        zero = jnp.zeros((l, HEAD_DIM), F32)
        return jnp.stack([one, zero, one * k_scale, zero])
    half = HEAD_DIM // 4
    rows_n = l // GRID_W
    freqs = ROPE_BASE ** (-jnp.arange(half, dtype=F32) / half)
    ang_r = jnp.arange(rows_n, dtype=F32)[:, None] * freqs
    ang_c = jnp.arange(GRID_W, dtype=F32)[:, None] * freqs
    cr, sr, cc, sn = lax.optimization_barrier((jnp.cos(ang_r), jnp.sin(ang_r), jnp.cos(ang_c), jnp.sin(ang_c)))
    cr, sr = jnp.repeat(cr, GRID_W, axis=0), jnp.repeat(sr, GRID_W, axis=0)
    cc, sn = jnp.tile(cc, (rows_n, 1)), jnp.tile(sn, (rows_n, 1))
    cos = jnp.concatenate([cr, cr, cc, cc], axis=-1)
    sin = jnp.concatenate([-sr, sr, -sn, sn], axis=-1)
    return jnp.stack([cos, sin, cos * k_scale, sin * k_scale])


def _retention_kernel(lg_ref, q_ref, k_ref, v_ref, f0_ref, g0_ref, y_ref, ffin_ref, gfin_ref,
                      f_ref, g_ref, gs_ref, mask_ref, dec_ref, bd_ref, *, hb, ns, cps):
    c = RET_CHUNK
    h0 = pl.program_id(1) * hb
    s = pl.program_id(2)

    @pl.when(s == 0)
    def _init():
        ii = lax.broadcasted_iota(jnp.int32, (c, c), 0)
        jj = lax.broadcasted_iota(jnp.int32, (c, c), 1)
        diff = (ii - jj).astype(F32)
        pos = lax.broadcasted_iota(jnp.int32, (c, HEAD_DIM), 0).astype(F32)
        for hh in range(hb):
            lgf = lg_ref[0, h0 + hh]
            lgb = lg_ref[1, h0 + hh]
            mask_ref[hh] = (jnp.where(diff >= 0, jnp.exp(lgf * jnp.maximum(diff, 0.0)), 0.0)
                            + jnp.where(diff <= 0, jnp.exp(lgb * jnp.maximum(-diff, 0.0)), 0.0))
            dec_ref[hh, 0] = jnp.exp(lgf * (pos + 1.0))
            dec_ref[hh, 1] = jnp.exp(lgb * (c - pos))
            dec_ref[hh, 2] = jnp.exp(lgf * (c - 1.0 - pos))
            dec_ref[hh, 3] = jnp.exp(lgb * pos)
            bd_ref[hh, 0] = jnp.exp(jnp.full((HEAD_DIM, HEAD_DIM), lgf * c, F32))
            bd_ref[hh, 1] = jnp.exp(jnp.full((HEAD_DIM, HEAD_DIM), lgb * c, F32))
            f_ref[hh] = f0_ref[0, hh]
            g_ref[hh] = g0_ref[0, hh]

    tn_dims = (((0,), (0,)), ((), ()))
    nt_dims = (((1,), (1,)), ((), ()))

    @pl.when(s < ns)
    def _backward():
        def body(t, carry):
            j = cps - 1 - t
            n = (ns - 1 - s) * cps + j
            r = pl.multiple_of(j * c, c)
            for hh in range(hb):
                gs_ref[n, hh] = g_ref[hh].astype(BF16)
                k = k_ref[0, hh, pl.ds(r, c), :]
                v = v_ref[0, hh, pl.ds(r, c), :]
                kb = (k.astype(F32) * dec_ref[hh, 3]).astype(BF16)
                upd = lax.dot_general(kb, v, tn_dims, preferred_element_type=F32)
                g_ref[hh] = g_ref[hh] * bd_ref[hh, 1] + upd
            return carry

        lax.fori_loop(0, cps, body, 0, unroll=min(2, cps))

        @pl.when(s == ns - 1)
        def _():
            for hh in range(hb):
                gfin_ref[0, hh] = g_ref[hh]

    @pl.when(s >= ns)
    def _forward():
        def body(j, carry):
            n = (s - ns) * cps + j
            r = pl.multiple_of(j * c, c)
            for hh in range(hb):
                q = q_ref[0, hh, pl.ds(r, c), :]
                k = k_ref[0, hh, pl.ds(r, c), :]
                v = v_ref[0, hh, pl.ds(r, c), :]
                q32 = q.astype(F32)
                scores = lax.dot_general(q, k, nt_dims, preferred_element_type=F32)
                p = (scores * mask_ref[hh]).astype(BF16)
                qd = jnp.concatenate([(q32 * dec_ref[hh, 0]).astype(BF16),
                                      (q32 * dec_ref[hh, 1]).astype(BF16)], axis=1)
                st = jnp.concatenate([f_ref[hh].astype(BF16), gs_ref[n, hh]], axis=0)
                o = (jnp.dot(p, v, preferred_element_type=F32)
                     + jnp.dot(qd, st, preferred_element_type=F32))
                mu = jnp.mean(o, axis=-1, keepdims=True)
                oc = o - mu
                var = jnp.mean(oc * oc, axis=-1, keepdims=True)
                y_ref[0, hh, pl.ds(r, c), :] = (oc * lax.rsqrt(var + EPS)).astype(BF16)
                kf = (k.astype(F32) * dec_ref[hh, 2]).astype(BF16)
                upd = lax.dot_general(kf, v, tn_dims, preferred_element_type=F32)
                f_ref[hh] = f_ref[hh] * bd_ref[hh, 0] + upd
            return carry

        lax.fori_loop(0, cps, body, 0, unroll=min(2, cps))

        @pl.when(s == 2 * ns - 1)
        def _():
            for hh in range(hb):
                ffin_ref[0, hh] = f_ref[hh]


def _retention(qkv, lg, f0, g0):
    b, _, l, d = qkv.shape
    c = RET_CHUNK
    hb = RET_HEADS_PER_STEP
    sc = min(2048, l)
    ns = l // sc
    cps = sc // c

    def kv_block(s):
        return jnp.where(s < ns, ns - 1 - s, s - ns)

    def q_block(s):
        return jnp.maximum(s - ns, 0)

    hblocks = HEADS // hb
    state_spec = pl.BlockSpec((1, hb, d, d), lambda i, h, s: (i, h, 0, 0))
    return pl.pallas_call(
        functools.partial(_retention_kernel, hb=hb, ns=ns, cps=cps),
        grid=(b, hblocks, 2 * ns),
        in_specs=[pl.BlockSpec(memory_space=pltpu.SMEM),
                  pl.BlockSpec((1, hb, sc, d), lambda i, h, s: (i, h, q_block(s), 0)),
                  pl.BlockSpec((1, hb, sc, d), lambda i, h, s: (i, hblocks + h, kv_block(s), 0)),
                  pl.BlockSpec((1, hb, sc, d), lambda i, h, s: (i, 2 * hblocks + h, kv_block(s), 0)),
                  state_spec, state_spec],
        out_specs=[pl.BlockSpec((1, hb, sc, d), lambda i, h, s: (i, h, q_block(s), 0)),
                   state_spec, state_spec],
        out_shape=[jax.ShapeDtypeStruct((b, HEADS, l, d), BF16),
                   jax.ShapeDtypeStruct((b, HEADS, d, d), F32),
                   jax.ShapeDtypeStruct((b, HEADS, d, d), F32)],
        scratch_shapes=[pltpu.VMEM((hb, d, d), F32),
                        pltpu.VMEM((hb, d, d), F32),
                        pltpu.VMEM((l // c, hb, d, d), BF16),
                        pltpu.VMEM((hb, c, c), F32),
                        pltpu.VMEM((hb, 4, c, d), F32),
                        pltpu.VMEM((hb, 2, d, d), F32)],
        compiler_params=_params("parallel", "parallel", "arbitrary"),
        name="retention",
    )(lg, qkv, qkv, qkv, f0, g0)


def _tap_plan(kernel_size):
    offsets = [HALO - kernel_size // 2 + j for j in range(kernel_size)]
    phases = sorted({o % SUBLANES for o in offsets})
    return phases, [(phases.index(o % SUBLANES), o - o % SUBLANES) for o in offsets]


def _shifted_copies(ph_ref, phases, tm):
    n = tm + 2 * HALO - SUBLANES
    for slot, p in enumerate(phases):
        if p:
            ph_ref[slot, 0:n, :] = ph_ref[0, p:p + n, :]


def _mix_kernel(y_ref, rest_ref, prev_ref, next_ref, x_ref, gret_ref, cw_ref, cb_ref, lng_ref,
                lnb_ref, sw_ref, wout_ref, gpost_ref, gate_ref, o_ref,
                mixed_ref, uph_ref, zph_ref, proj_ref, *, tm):
    m = pl.program_id(1)
    last = pl.num_programs(1) - 1
    c_a, c_b = RET_W, RET_W + CONF_W
    c_x, c_bb, c_c = RET_W + 2 * CONF_W, RET_W + 2 * CONF_W + SC_W, RET_W + 2 * CONF_W + 2 * SC_W
    conf_phases, conf_taps = _tap_plan(CONF_KERNEL)
    sc_phases, sc_taps = _tap_plan(SC_KERNEL)

    def glu(blk):
        a = blk[:, c_a:c_a + CONF_W].astype(F32)
        bgate = blk[:, c_b:c_b + CONF_W].astype(F32)
        return a * _sigmoid(bgate)

    def sc_in(blk):
        return blk[:, c_c:c_c + SC_W].astype(F32) * blk[:, c_x:c_x + SC_W].astype(F32)

    pv = prev_ref[0]
    nx = next_ref[0]
    uph_ref[0, 0:HALO, :] = jnp.where(m > 0, glu(pv), 0.0)
    zph_ref[0, 0:HALO, :] = jnp.where(m > 0, sc_in(pv), 0.0)
    uph_ref[0, HALO + tm:2 * HALO + tm, :] = jnp.where(m < last, glu(nx), 0.0)
    zph_ref[0, HALO + tm:2 * HALO + tm, :] = jnp.where(m < last, sc_in(nx), 0.0)

    gret = gret_ref[...]

    def gate_rows(r):
        blk = rest_ref[0, pl.ds(r, ROWS), :]
        gt = blk[:, 0:RET_W].astype(F32)
        y = jnp.concatenate([y_ref[0, h, pl.ds(r, ROWS), :] for h in range(HEADS)], axis=1)
        ret = y.astype(F32) * gret * (gt * _sigmoid(gt))
        mixed_ref[pl.ds(r, ROWS), 0:RET_W] = ret.astype(BF16)
        uph_ref[0, pl.ds(HALO + r, ROWS), :] = glu(blk)
        zph_ref[0, pl.ds(HALO + r, ROWS), :] = sc_in(blk)

    _row_loop(tm, gate_rows)
    _shifted_copies(uph_ref, conf_phases, tm)
    _shifted_copies(zph_ref, sc_phases, tm)

    cbias = cb_ref[...]
    lng = lng_ref[...]
    lnb = lnb_ref[...]

    def taps(ph_ref, w_ref, plan, r, init):
        groups = [init] * (ROWS // SUBLANES)
        for j, (slot, base) in enumerate(plan):
            w = w_ref[j]
            for i in range(len(groups)):
                rows = pl.ds(pl.multiple_of(r + base + i * SUBLANES, SUBLANES), SUBLANES)
                groups[i] = groups[i] + ph_ref[slot, rows, :] * w
        return jnp.concatenate(groups, axis=0)

    def conv_rows(r):
        proj_ref[pl.ds(r, ROWS), 0:CONF_W] = taps(uph_ref, cw_ref, conf_taps, r,
                                                  jnp.broadcast_to(cbias, (SUBLANES, CONF_W)))

    _row_loop(tm, conv_rows, unroll=1)

    def norm_rows(r):
        acc = proj_ref[pl.ds(r, ROWS), 0:CONF_W]
        mu = jnp.mean(acc, axis=-1, keepdims=True)
        ac = acc - mu
        var = jnp.mean(ac * ac, axis=-1, keepdims=True)
        u = ac * lax.rsqrt(var + EPS) * lng + lnb
        mixed_ref[pl.ds(r, ROWS), RET_W:RET_W + CONF_W] = (u * _sigmoid(u)).astype(BF16)
        zacc = taps(zph_ref, sw_ref, sc_taps, r, jnp.zeros((SUBLANES, SC_W), F32))
        scb = rest_ref[0, pl.ds(r, ROWS), c_bb:c_bb + SC_W].astype(F32)
        mixed_ref[pl.ds(r, ROWS), RET_W + CONF_W:RET_W + CONF_W + SC_W] = (scb * zacc).astype(BF16)

    _row_loop(tm, norm_rows)

    proj_ref[...] = jnp.dot(mixed_ref[...], wout_ref[...], preferred_element_type=F32)
    _gated_residual(x_ref, proj_ref, gpost_ref, gate_ref, o_ref, tm)


def _token_mix(y_ret, rest, x, ret_norm_g, conf_dw_w, conf_dw_b, conf_ln_g, conf_ln_b, sc_dw_w,
               w_out, g_post, gate):
    b, l, d = x.shape
    tm = min(512, l)
    hpt = tm // HALO
    nhalo = l // HALO
    row = lambda a: a.reshape(1, -1)
    tile_rows = lambda w: jnp.broadcast_to(w[:, None, :], (w.shape[0], SUBLANES, w.shape[1]))
    const = lambda shape: pl.BlockSpec(shape, lambda i, m: (0,) * len(shape))
    return pl.pallas_call(
        functools.partial(_mix_kernel, tm=tm),
        grid=(b, l // tm),
        in_specs=[pl.BlockSpec((1, HEADS, tm, HEAD_DIM), lambda i, m: (i, 0, m, 0)),
                  pl.BlockSpec((1, tm, REST_W), lambda i, m: (i, m, 0)),
                  pl.BlockSpec((1, HALO, REST_W), lambda i, m: (i, jnp.maximum(m * hpt - 1, 0), 0)),
                  pl.BlockSpec((1, HALO, REST_W),
                               lambda i, m: (i, jnp.minimum((m + 1) * hpt, nhalo - 1), 0)),
                  pl.BlockSpec((1, tm, d), lambda i, m: (i, m, 0)),
                  const((1, RET_W)),
                  const((CONF_KERNEL, SUBLANES, CONF_W)),
                  const((1, CONF_W)), const((1, CONF_W)), const((1, CONF_W)),
                  const((SC_KERNEL, SUBLANES, SC_W)),
                  _resident(w_out.shape),
                  const((1, d)),
                  pl.BlockSpec((1, 1, d), lambda i, m: (i, 0, 0))],
        out_specs=pl.BlockSpec((1, tm, d), lambda i, m: (i, m, 0)),
        out_shape=jax.ShapeDtypeStruct((b, l, d), F32),
        scratch_shapes=[pltpu.VMEM((tm, RET_W + CONF_W + SC_W), BF16),
                        pltpu.VMEM((len(_tap_plan(CONF_KERNEL)[0]), tm + 2 * HALO, CONF_W), F32),
                        pltpu.VMEM((len(_tap_plan(SC_KERNEL)[0]), tm + 2 * HALO, SC_W), F32),
                        pltpu.VMEM((tm, d), F32)],
        compiler_params=_params("parallel", "parallel"),
        name="token_mix",
    )(y_ret, rest, rest, rest, x, row(ret_norm_g), tile_rows(conf_dw_w), row(conf_dw_b),
      row(conf_ln_g), row(conf_ln_b), tile_rows(sc_dw_w), w_out, row(g_post), gate)


def _mlp_kernel(x_ref, gpre_ref, sh_ref, sc_ref, w1_ref, w2_ref, gpost_ref, gate_ref, o_ref, hb_ref,
                *, tm):
    f = pl.program_id(2)

    @pl.when(f == 0)
    def _prologue():
        _norm_modulate(x_ref, gpre_ref, sh_ref, sc_ref, hb_ref, tm)
        o_ref[...] = jnp.zeros(o_ref.shape, F32)

    h1 = jnp.dot(hb_ref[...], w1_ref[...], preferred_element_type=F32)
    a = jnp.maximum(h1, 0.0)
    o_ref[0] += jnp.dot((a * a).astype(BF16), w2_ref[...], preferred_element_type=F32)

    @pl.when(f == pl.num_programs(2) - 1)
    def _epilogue():
        _gated_residual(x_ref, o_ref.at[0], gpost_ref, gate_ref, o_ref, tm)


def _mlp(x, g_pre, shift, scale, w1, w2, g_post, gate):
    b, l, d = x.shape
    dff = w1.shape[1]
    tm = min(1024, l)
    tf = 512
    row = lambda a: a.reshape(1, -1)
    mod = pl.BlockSpec((1, 1, d), lambda i, m, f: (i, 0, 0))
    return pl.pallas_call(
        functools.partial(_mlp_kernel, tm=tm),
        grid=(b, l // tm, dff // tf),
        in_specs=[pl.BlockSpec((1, tm, d), lambda i, m, f: (i, m, 0)),
                  pl.BlockSpec((1, d), lambda i, m, f: (0, 0)),
                  mod, mod,
                  pl.BlockSpec((d, tf), lambda i, m, f: (0, f)),
                  pl.BlockSpec((tf, d), lambda i, m, f: (f, 0)),
                  pl.BlockSpec((1, d), lambda i, m, f: (0, 0)),
                  mod],
        out_specs=pl.BlockSpec((1, tm, d), lambda i, m, f: (i, m, 0)),
        out_shape=jax.ShapeDtypeStruct((b, l, d), F32),
        scratch_shapes=[pltpu.VMEM((tm, d), BF16)],
        compiler_params=_params("parallel", "parallel", "arbitrary"),
        name="mlp",
    )(x, row(g_pre), shift, scale, w1, w2, row(g_post), gate)


def kernel(x, c, ctx, c_ctx, w_ada, b_ada, g_pre_mix, g_post_mix, g_pre_ffn, g_post_ffn, w_in, ret_decay_fwd, ret_decay_bwd, ret_norm_g, conf_dw_w, conf_dw_b, conf_ln_g, conf_ln_b, sc_dw_w, w_out, w_ffn1, w_ffn2):
    depth = w_in.shape[0]
    b, l, d = x.shape
    lc = ctx.shape[1]

    cond = jnp.zeros((8, d), F32).at[:b].set(c).at[b].set(c_ctx)
    mods = _ada_modulation(cond, w_ada, b_ada)
    tab_x = _rope_tables(l, True)
    tab_c = _rope_tables(lc, False)
    zero_state = jnp.zeros((b, HEADS, HEAD_DIM, HEAD_DIM), F32)

    xc = ctx
    for layer in range(depth):
        last = layer == depth - 1
        mod_x = [t[:, None, :] for t in jnp.split(mods[layer, :b], 6, axis=-1)]
        mod_c = [jnp.broadcast_to(t[:, None, :], (b, 1, d))
                 for t in jnp.split(mods[layer, b:b + 1], 6, axis=-1)]
        lg = jnp.stack([jax.nn.log_sigmoid(ret_decay_fwd[layer].astype(F32)),
                        jax.nn.log_sigmoid(ret_decay_bwd[layer].astype(F32))])
        w_in_l = w_in[layer].astype(BF16)
        w_out_l = w_out[layer].astype(BF16)
        w1_l = w_ffn1[layer].astype(BF16)
        w2_l = w_ffn2[layer].astype(BF16)
        g_pre = g_pre_mix[layer].reshape(1, d)
        mix_params = (ret_norm_g[layer], conf_dw_w[layer], conf_dw_b[layer], conf_ln_g[layer],
                      conf_ln_b[layer], sc_dw_w[layer], w_out_l, g_post_mix[layer])
        mlp_params = (w1_l, w2_l, g_post_ffn[layer])

        qkv_c, rest_c = _inproj(xc, g_pre, mod_c[0], mod_c[1], tab_c, w_in_l)
        y_c, s_f, s_b = _retention(qkv_c, lg, zero_state, zero_state)

        qkv, rest = _inproj(x, g_pre, mod_x[0], mod_x[1], tab_x, w_in_l)
        y, _, _ = _retention(qkv, lg, s_f, s_b)
        x = _token_mix(y, rest, x, *mix_params, mod_x[2])
        x = _mlp(x, g_pre_ffn[layer], mod_x[3], mod_x[4], *mlp_params, mod_x[5])

        if not last:
            xc = _token_mix(y_c, rest_c, xc, *mix_params, mod_c[2])
            xc = _mlp(xc, g_pre_ffn[layer], mod_c[3], mod_c[4], *mlp_params, mod_c[5])
    return x
```

```python
import functools

import jax
import jax.numpy as jnp
from jax import lax
from jax.experimental import pallas as pl
from jax.experimental.pallas import tpu as pltpu

F32 = jnp.float32
BF16 = jnp.bfloat16

GRID_W = 64
HEADS = 8
HEAD_DIM = 128
RET_W = HEADS * HEAD_DIM
CONF_W = 512
CONF_KERNEL = 31
SC_W = 512
SC_KERNEL = 3
QKV_W = 3 * RET_W
REST_W = RET_W + 2 * CONF_W + 3 * SC_W
ROPE_BASE = 10000.0
EPS = 1e-6

LANES = 128
SUBLANES = 8
HALO = 16
ROWS = 32
RET_CHUNK = 256
RET_HEADS_PER_STEP = 2
VMEM_LIMIT = 58 * 1024 * 1024


def _rms(x, g):
    return x * lax.rsqrt(jnp.mean(x * x, axis=-1, keepdims=True) + EPS) * g


def _row_loop(tm, body, unroll=4):
    def step(i, carry):
        body(pl.multiple_of(i * ROWS, ROWS))
        return carry

    lax.fori_loop(0, tm // ROWS, step, 0, unroll=unroll)


def _norm_modulate(x_ref, g_ref, sh_ref, sc_ref, hb_ref, tm):
    gain = g_ref[...] * (1.0 + sc_ref[0])
    shift = sh_ref[0]

    def rows(r):
        hb_ref[pl.ds(r, ROWS), :] = (_rms(x_ref[0, pl.ds(r, ROWS), :], gain) + shift).astype(BF16)

    _row_loop(tm, rows)


def _gated_residual(x_ref, y_ref, gpost_ref, gate_ref, o_ref, tm):
    gain = gate_ref[0] * gpost_ref[...]

    def rows(r):
        o_ref[0, pl.ds(r, ROWS), :] = x_ref[0, pl.ds(r, ROWS), :] + _rms(y_ref[pl.ds(r, ROWS), :], gain)

    _row_loop(tm, rows)


def _sigmoid(x):
    return 1.0 / (1.0 + jnp.exp(-x))


def _params(*sem):
    return pltpu.CompilerParams(dimension_semantics=sem, vmem_limit_bytes=VMEM_LIMIT)


def _resident(shape):
    return pl.BlockSpec(shape, lambda *_: (0,) * len(shape), pipeline_mode=pl.Buffered(1))


def _ada_kernel(cond_ref, w_ref, b_ref, o_ref):
    s = cond_ref[...]
    s = s * _sigmoid(s)
    o_ref[0] = jnp.dot(s, w_ref[0], preferred_element_type=F32,
                       precision=lax.Precision.HIGHEST) + b_ref[0]


def _ada_modulation(cond, w_ada, b_ada):
    depth, d, n = w_ada.shape
    r = cond.shape[0]
    tn = 1024
    return pl.pallas_call(
        _ada_kernel,
        grid=(depth, n // tn),
        in_specs=[pl.BlockSpec((r, d), lambda l, j: (0, 0)),
                  pl.BlockSpec((1, d, tn), lambda l, j: (l, 0, j)),
                  pl.BlockSpec((1, 1, tn), lambda l, j: (l, 0, j))],
        out_specs=pl.BlockSpec((1, r, tn), lambda l, j: (l, 0, j)),
        out_shape=jax.ShapeDtypeStruct((depth, r, n), F32),
        compiler_params=_params("parallel", "parallel"),
        name="ada_modulation",
    )(cond, w_ada, b_ada.reshape(depth, 1, n))


def _inproj_kernel(x_ref, g_ref, sh_ref, sc_ref, tab_ref, w_ref, qkv_ref, rest_ref, hb_ref, *, tm):
    _norm_modulate(x_ref, g_ref, sh_ref, sc_ref, hb_ref, tm)
    hb = hb_ref[...]

    lane = lax.broadcasted_iota(jnp.int32, (tm, LANES), 1)
    first_of_pair = (lane & 32) == 0
    nt = 512
    for n in range((QKV_W + REST_W) // nt):
        a = jnp.dot(hb, w_ref[:, n * nt:(n + 1) * nt], preferred_element_type=F32)
        if n * nt < QKV_W:
            for j in range(nt // LANES):
                head = (n * nt) // LANES + j
                blk = a[:, j * LANES:(j + 1) * LANES]
                if head < 2 * HEADS:
                    t = 0 if head < HEADS else 2
                    swapped = jnp.where(first_of_pair, pltpu.roll(blk, 96, 1), pltpu.roll(blk, 32, 1))
                    blk = blk * tab_ref[t] + swapped * tab_ref[t + 1]
                qkv_ref[0, head] = blk.astype(BF16)
        else:
            c0 = n * nt - QKV_W
            rest_ref[0, :, c0:c0 + nt] = a.astype(BF16)


def _inproj(x, g_pre, shift, scale, tables, w_in):
    b, l, d = x.shape
    tm = min(512, l)
    return pl.pallas_call(
        functools.partial(_inproj_kernel, tm=tm),
        grid=(b, l // tm),
        in_specs=[pl.BlockSpec((1, tm, d), lambda i, m: (i, m, 0)),
                  pl.BlockSpec((1, d), lambda i, m: (0, 0)),
                  pl.BlockSpec((1, 1, d), lambda i, m: (i, 0, 0)),
                  pl.BlockSpec((1, 1, d), lambda i, m: (i, 0, 0)),
                  pl.BlockSpec((4, tm, LANES), lambda i, m: (0, m, 0)),
                  _resident(w_in.shape)],
        out_specs=[pl.BlockSpec((1, 3 * HEADS, tm, HEAD_DIM), lambda i, m: (i, 0, m, 0)),
                   pl.BlockSpec((1, tm, REST_W), lambda i, m: (i, m, 0))],
        out_shape=[jax.ShapeDtypeStruct((b, 3 * HEADS, l, HEAD_DIM), BF16),
                   jax.ShapeDtypeStruct((b, l, REST_W), BF16)],
        scratch_shapes=[pltpu.VMEM((tm, d), BF16)],
        compiler_params=_params("parallel", "parallel"),
        name="in_projection",
    )(x, g_pre, shift, scale, tables, w_in)


def _rope_tables(l, rotate):
    k_scale = HEAD_DIM ** -0.5
    if not rotate:
        one = jnp.ones((l, HEAD_DIM), F32)
        zero = jnp.zeros((l, HEAD_DIM), F32)
        return jnp.stack([one, zero, one * k_scale, zero])
    half = HEAD_DIM // 4
    rows_n = l // GRID_W
    freqs = ROPE_BASE ** (-jnp.arange(half, dtype=F32) / half)
    ang_r = jnp.arange(rows_n, dtype=F32)[:, None] * freqs
    ang_c = jnp.arange(GRID_W, dtype=F32)[:, None] * freqs
    cr, sr, cc, sn = lax.optimization_barrier((jnp.cos(ang_r), jnp.sin(ang_r), jnp.cos(ang_c), jnp.sin(ang_c)))
    cr, sr = jnp.repeat(cr, GRID_W, axis=0), jnp.repeat(sr, GRID_W, axis=0)
    cc, sn = jnp.tile(cc, (rows_n, 1)), jnp.tile(sn, (rows_n, 1))
    cos = jnp.concatenate([cr, cr, cc, cc], axis=-1)
    sin = jnp.concatenate([-sr, sr, -sn, sn], axis=-1)
    return jnp.stack([cos, sin, cos * k_scale, sin * k_scale])


def _retention_kernel(lg_ref, q_ref, k_ref, v_ref, f0_ref, g0_ref, y_ref, ffin_ref, gfin_ref,
                      f_ref, g_ref, gs_ref, mask_ref, dec_ref, bd_ref, *, hb, ns, cps):
    c = RET_CHUNK
    h0 = pl.program_id(1) * hb
    s = pl.program_id(2)

    @pl.when(s == 0)
    def _init():
        ii = lax.broadcasted_iota(jnp.int32, (c, c), 0)
        jj = lax.broadcasted_iota(jnp.int32, (c, c), 1)
        diff = (ii - jj).astype(F32)
        pos = lax.broadcasted_iota(jnp.int32, (c, HEAD_DIM), 0).astype(F32)
        for hh in range(hb):
            lgf = lg_ref[0, h0 + hh]
            lgb = lg_ref[1, h0 + hh]
            mask_ref[hh] = (jnp.where(diff >= 0, jnp.exp(lgf * jnp.maximum(diff, 0.0)), 0.0)
                            + jnp.where(diff <= 0, jnp.exp(lgb * jnp.maximum(-diff, 0.0)), 0.0))
            dec_ref[hh, 0] = jnp.exp(lgf * (pos + 1.0))
            dec_ref[hh, 1] = jnp.exp(lgb * (c - pos))
            dec_ref[hh, 2] = jnp.exp(lgf * (c - 1.0 - pos))
            dec_ref[hh, 3] = jnp.exp(lgb * pos)
            bd_ref[hh, 0] = jnp.exp(jnp.full((HEAD_DIM, HEAD_DIM), lgf * c, F32))
            bd_ref[hh, 1] = jnp.exp(jnp.full((HEAD_DIM, HEAD_DIM), lgb * c, F32))
            f_ref[hh] = f0_ref[0, hh]
            g_ref[hh] = g0_ref[0, hh]

    tn_dims = (((0,), (0,)), ((), ()))
    nt_dims = (((1,), (1,)), ((), ()))

    @pl.when(s < ns)
    def _backward():
        def body(t, carry):
            j = cps - 1 - t
            n = (ns - 1 - s) * cps + j
            r = pl.multiple_of(j * c, c)
            for hh in range(hb):
                gs_ref[n, hh] = g_ref[hh].astype(BF16)
                k = k_ref[0, hh, pl.ds(r, c), :]
                v = v_ref[0, hh, pl.ds(r, c), :]
                kb = (k.astype(F32) * dec_ref[hh, 3]).astype(BF16)
                upd = lax.dot_general(kb, v, tn_dims, preferred_element_type=F32)
                g_ref[hh] = g_ref[hh] * bd_ref[hh, 1] + upd
            return carry

        lax.fori_loop(0, cps, body, 0, unroll=min(8, cps))

        @pl.when(s == ns - 1)
        def _():
            for hh in range(hb):
                gfin_ref[0, hh] = g_ref[hh]

    @pl.when(s >= ns)
    def _forward():
        def body(j, carry):
            n = (s - ns) * cps + j
            r = pl.multiple_of(j * c, c)
            for hh in range(hb):
                q = q_ref[0, hh, pl.ds(r, c), :]
                k = k_ref[0, hh, pl.ds(r, c), :]
                v = v_ref[0, hh, pl.ds(r, c), :]
                q32 = q.astype(F32)
                scores = lax.dot_general(q, k, nt_dims, preferred_element_type=F32)
                p = (scores * mask_ref[hh]).astype(BF16)
                qd = jnp.concatenate([(q32 * dec_ref[hh, 0]).astype(BF16),
                                      (q32 * dec_ref[hh, 1]).astype(BF16)], axis=1)
                st = jnp.concatenate([f_ref[hh].astype(BF16), gs_ref[n, hh]], axis=0)
                o = (jnp.dot(p, v, preferred_element_type=F32)
                     + jnp.dot(qd, st, preferred_element_type=F32))
                mu = jnp.mean(o, axis=-1, keepdims=True)
                oc = o - mu
                var = jnp.mean(oc * oc, axis=-1, keepdims=True)
                y_ref[0, hh, pl.ds(r, c), :] = (oc * lax.rsqrt(var + EPS)).astype(BF16)
                kf = (k.astype(F32) * dec_ref[hh, 2]).astype(BF16)
                upd = lax.dot_general(kf, v, tn_dims, preferred_element_type=F32)
                f_ref[hh] = f_ref[hh] * bd_ref[hh, 0] + upd
            return carry

        lax.fori_loop(0, cps, body, 0, unroll=min(8, cps))

        @pl.when(s == 2 * ns - 1)
        def _():
            for hh in range(hb):
                ffin_ref[0, hh] = f_ref[hh]


def _retention(qkv, lg, f0, g0):
    b, _, l, d = qkv.shape
    c = RET_CHUNK
    hb = RET_HEADS_PER_STEP
    sc = min(2048, l)
    ns = l // sc
    cps = sc // c

    def kv_block(s):
        return jnp.where(s < ns, ns - 1 - s, s - ns)

    def q_block(s):
        return jnp.maximum(s - ns, 0)

    hblocks = HEADS // hb
    state_spec = pl.BlockSpec((1, hb, d, d), lambda i, h, s: (i, h, 0, 0))
    return pl.pallas_call(
        functools.partial(_retention_kernel, hb=hb, ns=ns, cps=cps),
        grid=(b, hblocks, 2 * ns),
        in_specs=[pl.BlockSpec(memory_space=pltpu.SMEM),
                  pl.BlockSpec((1, hb, sc, d), lambda i, h, s: (i, h, q_block(s), 0)),
                  pl.BlockSpec((1, hb, sc, d), lambda i, h, s: (i, hblocks + h, kv_block(s), 0)),
                  pl.BlockSpec((1, hb, sc, d), lambda i, h, s: (i, 2 * hblocks + h, kv_block(s), 0)),
                  state_spec, state_spec],
        out_specs=[pl.BlockSpec((1, hb, sc, d), lambda i, h, s: (i, h, q_block(s), 0)),
                   state_spec, state_spec],
        out_shape=[jax.ShapeDtypeStruct((b, HEADS, l, d), BF16),
                   jax.ShapeDtypeStruct((b, HEADS, d, d), F32),
                   jax.ShapeDtypeStruct((b, HEADS, d, d), F32)],
        scratch_shapes=[pltpu.VMEM((hb, d, d), F32),
                        pltpu.VMEM((hb, d, d), F32),
                        pltpu.VMEM((l // c, hb, d, d), BF16),
                        pltpu.VMEM((hb, c, c), F32),
                        pltpu.VMEM((hb, 4, c, d), F32),
                        pltpu.VMEM((hb, 2, d, d), F32)],
        compiler_params=_params("parallel", "parallel", "arbitrary"),
        name="retention",
    )(lg, qkv, qkv, qkv, f0, g0)


def _tap_plan(kernel_size):
    offsets = [HALO - kernel_size // 2 + j for j in range(kernel_size)]
    phases = sorted({o % SUBLANES for o in offsets})
    return phases, [(phases.index(o % SUBLANES), o - o % SUBLANES) for o in offsets]


def _shifted_copies(ph_ref, phases, tm):
    n = tm + 2 * HALO - SUBLANES
    for slot, p in enumerate(phases):
        if p:
            ph_ref[slot, 0:n, :] = ph_ref[0, p:p + n, :]


def _mix_kernel(y_ref, rest_ref, prev_ref, next_ref, x_ref, gret_ref, cw_ref, cb_ref, lng_ref,
                lnb_ref, sw_ref, wout_ref, gpost_ref, gate_ref, o_ref,
                mixed_ref, uph_ref, zph_ref, conv_ref, proj_ref, *, tm):
    m = pl.program_id(1)
    last = pl.num_programs(1) - 1
    c_a, c_b = RET_W, RET_W + CONF_W
    c_x, c_bb, c_c = RET_W + 2 * CONF_W, RET_W + 2 * CONF_W + SC_W, RET_W + 2 * CONF_W + 2 * SC_W
    conf_phases, conf_taps = _tap_plan(CONF_KERNEL)
    sc_phases, sc_taps = _tap_plan(SC_KERNEL)

    def glu(blk):
        a = blk[:, c_a:c_a + CONF_W].astype(F32)
        bgate = blk[:, c_b:c_b + CONF_W].astype(F32)
        return a * _sigmoid(bgate)

    def sc_in(blk):
        return blk[:, c_c:c_c + SC_W].astype(F32) * blk[:, c_x:c_x + SC_W].astype(F32)

    pv = prev_ref[0]
    nx = next_ref[0]
    uph_ref[0, 0:HALO, :] = jnp.where(m > 0, glu(pv), 0.0)
    zph_ref[0, 0:HALO, :] = jnp.where(m > 0, sc_in(pv), 0.0)
    uph_ref[0, HALO + tm:2 * HALO + tm, :] = jnp.where(m < last, glu(nx), 0.0)
    zph_ref[0, HALO + tm:2 * HALO + tm, :] = jnp.where(m < last, sc_in(nx), 0.0)

    gret = gret_ref[...]

    def gate_rows(r):
        blk = rest_ref[0, pl.ds(r, ROWS), :]
        gt = blk[:, 0:RET_W].astype(F32)
        y = jnp.concatenate([y_ref[0, h, pl.ds(r, ROWS), :] for h in range(HEADS)], axis=1)
        ret = y.astype(F32) * gret * (gt * _sigmoid(gt))
        mixed_ref[pl.ds(r, ROWS), 0:RET_W] = ret.astype(BF16)
        uph_ref[0, pl.ds(HALO + r, ROWS), :] = glu(blk)
        zph_ref[0, pl.ds(HALO + r, ROWS), :] = sc_in(blk)

    _row_loop(tm, gate_rows)
    _shifted_copies(uph_ref, conf_phases, tm)
    _shifted_copies(zph_ref, sc_phases, tm)

    cbias = cb_ref[...]
    lng = lng_ref[...]
    lnb = lnb_ref[...]

    def taps(ph_ref, w_ref, plan, r, init):
        groups = [init] * (ROWS // SUBLANES)
        for j, (slot, base) in enumerate(plan):
            w = w_ref[j]
            for i in range(len(groups)):
                rows = pl.ds(pl.multiple_of(r + base + i * SUBLANES, SUBLANES), SUBLANES)
                groups[i] = groups[i] + ph_ref[slot, rows, :] * w
        return jnp.concatenate(groups, axis=0)

    def conv_rows(r):
        conv_ref[pl.ds(r, ROWS), :] = taps(uph_ref, cw_ref, conf_taps, r,
                                           jnp.broadcast_to(cbias, (SUBLANES, CONF_W)))

    _row_loop(tm, conv_rows, unroll=1)

    def norm_rows(r):
        acc = conv_ref[pl.ds(r, ROWS), :]
        mu = jnp.mean(acc, axis=-1, keepdims=True)
        ac = acc - mu
        var = jnp.mean(ac * ac, axis=-1, keepdims=True)
        u = ac * lax.rsqrt(var + EPS) * lng + lnb
        mixed_ref[pl.ds(r, ROWS), RET_W:RET_W + CONF_W] = (u * _sigmoid(u)).astype(BF16)
        zacc = taps(zph_ref, sw_ref, sc_taps, r, jnp.zeros((SUBLANES, SC_W), F32))
        scb = rest_ref[0, pl.ds(r, ROWS), c_bb:c_bb + SC_W].astype(F32)
        mixed_ref[pl.ds(r, ROWS), RET_W + CONF_W:RET_W + CONF_W + SC_W] = (scb * zacc).astype(BF16)

    _row_loop(tm, norm_rows)

    proj_ref[...] = jnp.dot(mixed_ref[...], wout_ref[...], preferred_element_type=F32)
    _gated_residual(x_ref, proj_ref, gpost_ref, gate_ref, o_ref, tm)


def _token_mix(y_ret, rest, x, ret_norm_g, conf_dw_w, conf_dw_b, conf_ln_g, conf_ln_b, sc_dw_w,
               w_out, g_post, gate):
    b, l, d = x.shape
    tm = min(512, l)
    hpt = tm // HALO
    nhalo = l // HALO
    row = lambda a: a.reshape(1, -1)
    tile_rows = lambda w: jnp.broadcast_to(w[:, None, :], (w.shape[0], SUBLANES, w.shape[1]))
    const = lambda shape: pl.BlockSpec(shape, lambda i, m: (0,) * len(shape))
    return pl.pallas_call(
        functools.partial(_mix_kernel, tm=tm),
        grid=(b, l // tm),
        in_specs=[pl.BlockSpec((1, HEADS, tm, HEAD_DIM), lambda i, m: (i, 0, m, 0)),
                  pl.BlockSpec((1, tm, REST_W), lambda i, m: (i, m, 0)),
                  pl.BlockSpec((1, HALO, REST_W), lambda i, m: (i, jnp.maximum(m * hpt - 1, 0), 0)),
                  pl.BlockSpec((1, HALO, REST_W),
                               lambda i, m: (i, jnp.minimum((m + 1) * hpt, nhalo - 1), 0)),
                  pl.BlockSpec((1, tm, d), lambda i, m: (i, m, 0)),
                  const((1, RET_W)),
                  const((CONF_KERNEL, SUBLANES, CONF_W)),
                  const((1, CONF_W)), const((1, CONF_W)), const((1, CONF_W)),
                  const((SC_KERNEL, SUBLANES, SC_W)),
                  _resident(w_out.shape),
                  const((1, d)),
                  pl.BlockSpec((1, 1, d), lambda i, m: (i, 0, 0))],
        out_specs=pl.BlockSpec((1, tm, d), lambda i, m: (i, m, 0)),
        out_shape=jax.ShapeDtypeStruct((b, l, d), F32),
        scratch_shapes=[pltpu.VMEM((tm, RET_W + CONF_W + SC_W), BF16),
                        pltpu.VMEM((len(_tap_plan(CONF_KERNEL)[0]), tm + 2 * HALO, CONF_W), F32),
                        pltpu.VMEM((len(_tap_plan(SC_KERNEL)[0]), tm + 2 * HALO, SC_W), F32),
                        pltpu.VMEM((tm, CONF_W), F32),
                        pltpu.VMEM((tm, d), F32)],
        compiler_params=_params("parallel", "parallel"),
        name="token_mix",
    )(y_ret, rest, rest, rest, x, row(ret_norm_g), tile_rows(conf_dw_w), row(conf_dw_b),
      row(conf_ln_g), row(conf_ln_b), tile_rows(sc_dw_w), w_out, row(g_post), gate)


def _mlp_kernel(x_ref, gpre_ref, sh_ref, sc_ref, w1_ref, w2_ref, gpost_ref, gate_ref, o_ref, hb_ref,
                acc_ref, *, tm):
    f = pl.program_id(2)

    @pl.when(f == 0)
    def _prologue():
        _norm_modulate(x_ref, gpre_ref, sh_ref, sc_ref, hb_ref, tm)
        acc_ref[...] = jnp.zeros(acc_ref.shape, F32)

    h1 = jnp.dot(hb_ref[...], w1_ref[...], preferred_element_type=F32)
    a = jnp.maximum(h1, 0.0)
    acc_ref[...] += jnp.dot((a * a).astype(BF16), w2_ref[...], preferred_element_type=F32)

    @pl.when(f == pl.num_programs(2) - 1)
    def _epilogue():
        _gated_residual(x_ref, acc_ref, gpost_ref, gate_ref, o_ref, tm)


def _mlp(x, g_pre, shift, scale, w1, w2, g_post, gate):
    b, l, d = x.shape
    dff = w1.shape[1]
    tm = min(1024, l)
    tf = 512
    row = lambda a: a.reshape(1, -1)
    mod = pl.BlockSpec((1, 1, d), lambda i, m, f: (i, 0, 0))
    return pl.pallas_call(
        functools.partial(_mlp_kernel, tm=tm),
        grid=(b, l // tm, dff // tf),
        in_specs=[pl.BlockSpec((1, tm, d), lambda i, m, f: (i, m, 0)),
                  pl.BlockSpec((1, d), lambda i, m, f: (0, 0)),
                  mod, mod,
                  pl.BlockSpec((d, tf), lambda i, m, f: (0, f)),
                  pl.BlockSpec((tf, d), lambda i, m, f: (f, 0)),
                  pl.BlockSpec((1, d), lambda i, m, f: (0, 0)),
                  mod],
        out_specs=pl.BlockSpec((1, tm, d), lambda i, m, f: (i, m, 0)),
        out_shape=jax.ShapeDtypeStruct((b, l, d), F32),
        scratch_shapes=[pltpu.VMEM((tm, d), BF16), pltpu.VMEM((tm, d), F32)],
        compiler_params=_params("parallel", "parallel", "arbitrary"),
        name="mlp",
    )(x, row(g_pre), shift, scale, w1, w2, row(g_post), gate)


def kernel(x, c, ctx, c_ctx, w_ada, b_ada, g_pre_mix, g_post_mix, g_pre_ffn, g_post_ffn, w_in, ret_decay_fwd, ret_decay_bwd, ret_norm_g, conf_dw_w, conf_dw_b, conf_ln_g, conf_ln_b, sc_dw_w, w_out, w_ffn1, w_ffn2):
    depth = w_in.shape[0]
    b, l, d = x.shape
    lc = ctx.shape[1]

    cond = jnp.zeros((8, d), F32).at[:b].set(c).at[b].set(c_ctx)
    mods = _ada_modulation(cond, w_ada, b_ada)
    tab_x = _rope_tables(l, True)
    tab_c = _rope_tables(lc, False)
    zero_state = jnp.zeros((b, HEADS, HEAD_DIM, HEAD_DIM), F32)

    xc = ctx
    for layer in range(depth):
        last = layer == depth - 1
        mod_x = [t[:, None, :] for t in jnp.split(mods[layer, :b], 6, axis=-1)]
        mod_c = [jnp.broadcast_to(t[:, None, :], (b, 1, d))
                 for t in jnp.split(mods[layer, b:b + 1], 6, axis=-1)]
        lg = jnp.stack([jax.nn.log_sigmoid(ret_decay_fwd[layer].astype(F32)),
                        jax.nn.log_sigmoid(ret_decay_bwd[layer].astype(F32))])
        w_in_l = w_in[layer].astype(BF16)
        w_out_l = w_out[layer].astype(BF16)
        w1_l = w_ffn1[layer].astype(BF16)
        w2_l = w_ffn2[layer].astype(BF16)
        g_pre = g_pre_mix[layer].reshape(1, d)
        mix_params = (ret_norm_g[layer], conf_dw_w[layer], conf_dw_b[layer], conf_ln_g[layer],
                      conf_ln_b[layer], sc_dw_w[layer], w_out_l, g_post_mix[layer])
        mlp_params = (w1_l, w2_l, g_post_ffn[layer])

        qkv_c, rest_c = _inproj(xc, g_pre, mod_c[0], mod_c[1], tab_c, w_in_l)
        y_c, s_f, s_b = _retention(qkv_c, lg, zero_state, zero_state)

        qkv, rest = _inproj(x, g_pre, mod_x[0], mod_x[1], tab_x, w_in_l)
        y, _, _ = _retention(qkv, lg, s_f, s_b)
        x = _token_mix(y, rest, x, *mix_params, mod_x[2])
        x = _mlp(x, g_pre_ffn[layer], mod_x[3], mod_x[4], *mlp_params, mod_x[5])

        if not last:
            xc = _token_mix(y_c, rest_c, xc, *mix_params, mod_c[2])
            xc = _mlp(xc.reshape(1, b * lc, d), g_pre_ffn[layer], mod_c[3][:1], mod_c[4][:1],
                      *mlp_params, mod_c[5][:1]).reshape(b, lc, d)
    return x
```

```python
import functools

import jax
import jax.numpy as jnp
from jax import lax
from jax.experimental import pallas as pl
from jax.experimental.pallas import tpu as pltpu

F32 = jnp.float32
BF16 = jnp.bfloat16

GRID_W = 64
HEADS = 8
HEAD_DIM = 128
RET_W = HEADS * HEAD_DIM
CONF_W = 512
CONF_KERNEL = 31
SC_W = 512
SC_KERNEL = 3
QKV_W = 3 * RET_W
REST_W = RET_W + 2 * CONF_W + 3 * SC_W
ROPE_BASE = 10000.0
EPS = 1e-6

LANES = 128
SUBLANES = 8
HALO = 16
ROWS = 32
RET_CHUNK = 256
RET_HEADS_PER_STEP = 2
VMEM_LIMIT = 58 * 1024 * 1024


def _rms(x, g):
    return x * lax.rsqrt(jnp.mean(x * x, axis=-1, keepdims=True) + EPS) * g


def _row_loop(tm, body, unroll=4):
    def step(i, carry):
        body(pl.multiple_of(i * ROWS, ROWS))
        return carry

    lax.fori_loop(0, tm // ROWS, step, 0, unroll=unroll)


def _norm_modulate(x_ref, g_ref, sh_ref, sc_ref, hb_ref, tm):
    gain = g_ref[...] * (1.0 + sc_ref[0])
    shift = sh_ref[0]

    def rows(r):
        hb_ref[pl.ds(r, ROWS), :] = (_rms(x_ref[0, pl.ds(r, ROWS), :], gain) + shift).astype(BF16)

    _row_loop(tm, rows)


def _gated_residual(x_ref, y_ref, gpost_ref, gate_ref, o_ref, tm):
    gain = gate_ref[0] * gpost_ref[...]

    def rows(r):
        o_ref[0, pl.ds(r, ROWS), :] = x_ref[0, pl.ds(r, ROWS), :] + _rms(y_ref[pl.ds(r, ROWS), :], gain)

    _row_loop(tm, rows)


def _sigmoid(x):
    return 1.0 / (1.0 + jnp.exp(-x))


def _params(*sem):
    return pltpu.CompilerParams(dimension_semantics=sem, vmem_limit_bytes=VMEM_LIMIT)


def _resident(w_all, layer):
    return pl.BlockSpec((None,) + w_all.shape[1:], lambda *_: (layer, 0, 0),
                        pipeline_mode=pl.Buffered(1))


def _ada_kernel(cond_ref, w_ref, b_ref, o_ref):
    s = cond_ref[...]
    s = s * _sigmoid(s)
    o_ref[0] = jnp.dot(s, w_ref[0], preferred_element_type=F32,
                       precision=lax.Precision.HIGHEST) + b_ref[0]


def _ada_modulation(cond, w_ada, b_ada):
    depth, d, n = w_ada.shape
    r = cond.shape[0]
    tn = 1024
    return pl.pallas_call(
        _ada_kernel,
        grid=(depth, n // tn),
        in_specs=[pl.BlockSpec((r, d), lambda l, j: (0, 0)),
                  pl.BlockSpec((1, d, tn), lambda l, j: (l, 0, j)),
                  pl.BlockSpec((1, 1, tn), lambda l, j: (l, 0, j))],
        out_specs=pl.BlockSpec((1, r, tn), lambda l, j: (l, 0, j)),
        out_shape=jax.ShapeDtypeStruct((depth, r, n), F32),
        compiler_params=_params("parallel", "parallel"),
        name="ada_modulation",
    )(cond, w_ada, b_ada.reshape(depth, 1, n))


def _inproj_kernel(x_ref, g_ref, sh_ref, sc_ref, tab_ref, w_ref, qkv_ref, rest_ref, hb_ref, *, tm):
    _norm_modulate(x_ref, g_ref, sh_ref, sc_ref, hb_ref, tm)
    hb = hb_ref[...]

    lane = lax.broadcasted_iota(jnp.int32, (tm, LANES), 1)
    first_of_pair = (lane & 32) == 0
    nt = 512
    for n in range((QKV_W + REST_W) // nt):
        a = jnp.dot(hb, w_ref[:, n * nt:(n + 1) * nt], preferred_element_type=F32)
        if n * nt < QKV_W:
            for j in range(nt // LANES):
                head = (n * nt) // LANES + j
                blk = a[:, j * LANES:(j + 1) * LANES]
                if head < 2 * HEADS:
                    swapped = jnp.where(first_of_pair, pltpu.roll(blk, 96, 1), pltpu.roll(blk, 32, 1))
                    blk = blk * tab_ref[0] + swapped * tab_ref[1]
                if HEADS <= head < 2 * HEADS:
                    blk = blk * HEAD_DIM ** -0.5
                qkv_ref[0, head] = blk.astype(BF16)
        else:
            c0 = n * nt - QKV_W
            rest_ref[0, :, c0:c0 + nt] = a.astype(BF16)


def _inproj(x, g_pre, shift, scale, tables, w_in, layer):
    b, l, d = x.shape
    tm = min(512, l)
    return pl.pallas_call(
        functools.partial(_inproj_kernel, tm=tm),
        grid=(b, l // tm),
        in_specs=[pl.BlockSpec((1, tm, d), lambda i, m: (i, m, 0)),
                  pl.BlockSpec((1, d), lambda i, m: (0, 0)),
                  pl.BlockSpec((1, 1, d), lambda i, m: (i, 0, 0)),
                  pl.BlockSpec((1, 1, d), lambda i, m: (i, 0, 0)),
                  pl.BlockSpec((2, tm, LANES), lambda i, m: (0, m, 0)),
                  _resident(w_in, layer)],
        out_specs=[pl.BlockSpec((1, 3 * HEADS, tm, HEAD_DIM), lambda i, m: (i, 0, m, 0)),
                   pl.BlockSpec((1, tm, REST_W), lambda i, m: (i, m, 0))],
        out_shape=[jax.ShapeDtypeStruct((b, 3 * HEADS, l, HEAD_DIM), BF16),
                   jax.ShapeDtypeStruct((b, l, REST_W), BF16)],
        scratch_shapes=[pltpu.VMEM((tm, d), BF16)],
        compiler_params=_params("parallel", "parallel"),
        name="in_projection",
    )(x, g_pre, shift, scale, tables, w_in)


def _rope_tables(l, rotate):
    if not rotate:
        return jnp.stack([jnp.ones((l, HEAD_DIM), F32), jnp.zeros((l, HEAD_DIM), F32)])
    half = HEAD_DIM // 4
    rows_n = l // GRID_W
    freqs = ROPE_BASE ** (-jnp.arange(half, dtype=F32) / half)
    ang_r = jnp.arange(rows_n, dtype=F32)[:, None] * freqs
    ang_c = jnp.arange(GRID_W, dtype=F32)[:, None] * freqs
    cr, sr, cc, sn = lax.optimization_barrier((jnp.cos(ang_r), jnp.sin(ang_r), jnp.cos(ang_c), jnp.sin(ang_c)))
    cr, sr = jnp.repeat(cr, GRID_W, axis=0), jnp.repeat(sr, GRID_W, axis=0)
    cc, sn = jnp.tile(cc, (rows_n, 1)), jnp.tile(sn, (rows_n, 1))
    cos = jnp.concatenate([cr, cr, cc, cc], axis=-1)
    sin = jnp.concatenate([-sr, sr, -sn, sn], axis=-1)
    return jnp.stack([cos, sin])


def _retention_kernel(lg_ref, q_ref, k_ref, v_ref, f0_ref, g0_ref, y_ref, ffin_ref, gfin_ref,
                      f_ref, g_ref, gs_ref, mask_ref, dec_ref, bd_ref, *, hb, ns, cps):
    c = RET_CHUNK
    h0 = pl.program_id(1) * hb
    s = pl.program_id(2)

    @pl.when(s == 0)
    def _init():
        ii = lax.broadcasted_iota(jnp.int32, (c, c), 0)
        jj = lax.broadcasted_iota(jnp.int32, (c, c), 1)
        diff = (ii - jj).astype(F32)
        pos = lax.broadcasted_iota(jnp.int32, (c, HEAD_DIM), 0).astype(F32)
        for hh in range(hb):
            lgf = lg_ref[0, h0 + hh]
            lgb = lg_ref[1, h0 + hh]
            mask_ref[hh] = (jnp.where(diff >= 0, jnp.exp(lgf * jnp.maximum(diff, 0.0)), 0.0)
                            + jnp.where(diff <= 0, jnp.exp(lgb * jnp.maximum(-diff, 0.0)), 0.0))
            dec_ref[hh, 0] = jnp.exp(lgf * (pos + 1.0))
            dec_ref[hh, 1] = jnp.exp(lgb * (c - pos))
            dec_ref[hh, 2] = jnp.exp(lgf * (c - 1.0 - pos))
            dec_ref[hh, 3] = jnp.exp(lgb * pos)
            bd_ref[hh, 0] = jnp.exp(jnp.full((HEAD_DIM, HEAD_DIM), lgf * c, F32))
            bd_ref[hh, 1] = jnp.exp(jnp.full((HEAD_DIM, HEAD_DIM), lgb * c, F32))
            f_ref[hh] = f0_ref[0, hh]
            g_ref[hh] = g0_ref[0, hh]

    tn_dims = (((0,), (0,)), ((), ()))
    nt_dims = (((1,), (1,)), ((), ()))

    @pl.when(s < ns)
    def _backward():
        def body(t, carry):
            j = cps - 1 - t
            n = (ns - 1 - s) * cps + j
            r = pl.multiple_of(j * c, c)
            for hh in range(hb):
                gs_ref[n, hh] = g_ref[hh].astype(BF16)
                k = k_ref[0, hh, pl.ds(r, c), :]
                v = v_ref[0, hh, pl.ds(r, c), :]
                kb = (k.astype(F32) * dec_ref[hh, 3]).astype(BF16)
                upd = lax.dot_general(kb, v, tn_dims, preferred_element_type=F32)
                g_ref[hh] = g_ref[hh] * bd_ref[hh, 1] + upd
            return carry

        lax.fori_loop(0, cps, body, 0, unroll=min(8, cps))

        @pl.when(s == ns - 1)
        def _():
            for hh in range(hb):
                gfin_ref[0, hh] = g_ref[hh]

    @pl.when(s >= ns)
    def _forward():
        def body(j, carry):
            n = (s - ns) * cps + j
            r = pl.multiple_of(j * c, c)
            for hh in range(hb):
                q = q_ref[0, hh, pl.ds(r, c), :]
                k = k_ref[0, hh, pl.ds(r, c), :]
                v = v_ref[0, hh, pl.ds(r, c), :]
                q32 = q.astype(F32)
                scores = lax.dot_general(q, k, nt_dims, preferred_element_type=F32)
                p = (scores * mask_ref[hh]).astype(BF16)
                qd = jnp.concatenate([(q32 * dec_ref[hh, 0]).astype(BF16),
                                      (q32 * dec_ref[hh, 1]).astype(BF16)], axis=1)
                st = jnp.concatenate([f_ref[hh].astype(BF16), gs_ref[n, hh]], axis=0)
                o = (jnp.dot(p, v, preferred_element_type=F32)
                     + jnp.dot(qd, st, preferred_element_type=F32))
                mu = jnp.mean(o, axis=-1, keepdims=True)
                oc = o - mu
                var = jnp.mean(oc * oc, axis=-1, keepdims=True)
                y_ref[0, hh, pl.ds(r, c), :] = (oc * lax.rsqrt(var + EPS)).astype(BF16)
                kf = (k.astype(F32) * dec_ref[hh, 2]).astype(BF16)
                upd = lax.dot_general(kf, v, tn_dims, preferred_element_type=F32)
                f_ref[hh] = f_ref[hh] * bd_ref[hh, 0] + upd
            return carry

        lax.fori_loop(0, cps, body, 0, unroll=min(8, cps))

        @pl.when(s == 2 * ns - 1)
        def _():
            for hh in range(hb):
                ffin_ref[0, hh] = f_ref[hh]


def _retention(qkv, lg, f0, g0):
    b, _, l, d = qkv.shape
    c = RET_CHUNK
    hb = RET_HEADS_PER_STEP
    sc = min(4096, l)
    ns = l // sc
    cps = sc // c

    def kv_block(s):
        return jnp.where(s < ns, ns - 1 - s, s - ns)

    def q_block(s):
        return jnp.maximum(s - ns, 0)

    hblocks = HEADS // hb
    state_spec = pl.BlockSpec((1, hb, d, d), lambda i, h, s: (i, h, 0, 0))
    return pl.pallas_call(
        functools.partial(_retention_kernel, hb=hb, ns=ns, cps=cps),
        grid=(b, hblocks, 2 * ns),
        in_specs=[pl.BlockSpec(memory_space=pltpu.SMEM),
                  pl.BlockSpec((1, hb, sc, d), lambda i, h, s: (i, h, q_block(s), 0)),
                  pl.BlockSpec((1, hb, sc, d), lambda i, h, s: (i, hblocks + h, kv_block(s), 0)),
                  pl.BlockSpec((1, hb, sc, d), lambda i, h, s: (i, 2 * hblocks + h, kv_block(s), 0)),
                  state_spec, state_spec],
        out_specs=[pl.BlockSpec((1, hb, sc, d), lambda i, h, s: (i, h, q_block(s), 0)),
                   state_spec, state_spec],
        out_shape=[jax.ShapeDtypeStruct((b, HEADS, l, d), BF16),
                   jax.ShapeDtypeStruct((b, HEADS, d, d), F32),
                   jax.ShapeDtypeStruct((b, HEADS, d, d), F32)],
        scratch_shapes=[pltpu.VMEM((hb, d, d), F32),
                        pltpu.VMEM((hb, d, d), F32),
                        pltpu.VMEM((l // c, hb, d, d), BF16),
                        pltpu.VMEM((hb, c, c), F32),
                        pltpu.VMEM((hb, 4, c, d), F32),
                        pltpu.VMEM((hb, 2, d, d), F32)],
        compiler_params=_params("parallel", "parallel", "arbitrary"),
        name="retention",
    )(lg, qkv, qkv, qkv, f0, g0)


def _tap_plan(kernel_size):
    offsets = [HALO - kernel_size // 2 + j for j in range(kernel_size)]
    phases = sorted({o % SUBLANES for o in offsets})
    return phases, [(phases.index(o % SUBLANES), o - o % SUBLANES) for o in offsets]


def _shifted_copies(ph_ref, phases, tm):
    n = tm + 2 * HALO - SUBLANES
    for slot, p in enumerate(phases):
        if p:
            ph_ref[slot, 0:n, :] = ph_ref[0, p:p + n, :]


def _mix_kernel(y_ref, rest_ref, prev_ref, next_ref, x_ref, gret_ref, cw_ref, cb_ref, lng_ref,
                lnb_ref, sw_ref, wout_ref, gpost_ref, gate_ref, o_ref,
                mixed_ref, uph_ref, zph_ref, conv_ref, proj_ref, *, tm):
    m = pl.program_id(1)
    last = pl.num_programs(1) - 1
    c_a, c_b = RET_W, RET_W + CONF_W
    c_x, c_bb, c_c = RET_W + 2 * CONF_W, RET_W + 2 * CONF_W + SC_W, RET_W + 2 * CONF_W + 2 * SC_W
    conf_phases, conf_taps = _tap_plan(CONF_KERNEL)
    sc_phases, sc_taps = _tap_plan(SC_KERNEL)

    def glu(blk):
        a = blk[:, c_a:c_a + CONF_W].astype(F32)
        bgate = blk[:, c_b:c_b + CONF_W].astype(F32)
        return a * _sigmoid(bgate)

    def sc_in(blk):
        return blk[:, c_c:c_c + SC_W].astype(F32) * blk[:, c_x:c_x + SC_W].astype(F32)

    pv = prev_ref[0]
    nx = next_ref[0]
    uph_ref[0, 0:HALO, :] = jnp.where(m > 0, glu(pv), 0.0)
    zph_ref[0, 0:HALO, :] = jnp.where(m > 0, sc_in(pv), 0.0)
    uph_ref[0, HALO + tm:2 * HALO + tm, :] = jnp.where(m < last, glu(nx), 0.0)
    zph_ref[0, HALO + tm:2 * HALO + tm, :] = jnp.where(m < last, sc_in(nx), 0.0)

    gret = gret_ref[...]

    def gate_rows(r):
        blk = rest_ref[0, pl.ds(r, ROWS), :]
        gt = blk[:, 0:RET_W].astype(F32)
        y = jnp.concatenate([y_ref[0, h, pl.ds(r, ROWS), :] for h in range(HEADS)], axis=1)
        ret = y.astype(F32) * gret * (gt * _sigmoid(gt))
        mixed_ref[pl.ds(r, ROWS), 0:RET_W] = ret.astype(BF16)
        uph_ref[0, pl.ds(HALO + r, ROWS), :] = glu(blk)
        zph_ref[0, pl.ds(HALO + r, ROWS), :] = sc_in(blk)

    _row_loop(tm, gate_rows)
    _shifted_copies(uph_ref, conf_phases, tm)
    _shifted_copies(zph_ref, sc_phases, tm)

    cbias = cb_ref[...]
    lng = lng_ref[...]
    lnb = lnb_ref[...]

    def taps(ph_ref, w_ref, plan, r, init):
        groups = [init] * (ROWS // SUBLANES)
        for j, (slot, base) in enumerate(plan):
            w = w_ref[j]
            for i in range(len(groups)):
                rows = pl.ds(pl.multiple_of(r + base + i * SUBLANES, SUBLANES), SUBLANES)
                groups[i] = groups[i] + ph_ref[slot, rows, :] * w
        return jnp.concatenate(groups, axis=0)

    def conv_rows(r):
        conv_ref[pl.ds(r, ROWS), :] = taps(uph_ref, cw_ref, conf_taps, r,
                                           jnp.broadcast_to(cbias, (SUBLANES, CONF_W)))

    _row_loop(tm, conv_rows, unroll=1)

    def norm_rows(r):
        acc = conv_ref[pl.ds(r, ROWS), :]
        mu = jnp.mean(acc, axis=-1, keepdims=True)
        ac = acc - mu
        var = jnp.mean(ac * ac, axis=-1, keepdims=True)
        u = ac * lax.rsqrt(var + EPS) * lng + lnb
        mixed_ref[pl.ds(r, ROWS), RET_W:RET_W + CONF_W] = (u * _sigmoid(u)).astype(BF16)
        zacc = taps(zph_ref, sw_ref, sc_taps, r, jnp.zeros((SUBLANES, SC_W), F32))
        scb = rest_ref[0, pl.ds(r, ROWS), c_bb:c_bb + SC_W].astype(F32)
        mixed_ref[pl.ds(r, ROWS), RET_W + CONF_W:RET_W + CONF_W + SC_W] = (scb * zacc).astype(BF16)

    _row_loop(tm, norm_rows)

    proj_ref[...] = jnp.dot(mixed_ref[...], wout_ref[...], preferred_element_type=F32)
    _gated_residual(x_ref, proj_ref, gpost_ref, gate_ref, o_ref, tm)


def _token_mix(y_ret, rest, x, ret_norm_g, conf_dw_w, conf_dw_b, conf_ln_g, conf_ln_b, sc_dw_w,
               w_out, g_post, gate, layer):
    b, l, d = x.shape
    tm = min(512, l)
    hpt = tm // HALO
    nhalo = l // HALO
    row = lambda a: a.reshape(1, -1)
    tile_rows = lambda w: jnp.broadcast_to(w[:, None, :], (w.shape[0], SUBLANES, w.shape[1]))
    const = lambda shape: pl.BlockSpec(shape, lambda i, m: (0,) * len(shape))
    return pl.pallas_call(
        functools.partial(_mix_kernel, tm=tm),
        grid=(b, l // tm),
        in_specs=[pl.BlockSpec((1, HEADS, tm, HEAD_DIM), lambda i, m: (i, 0, m, 0)),
                  pl.BlockSpec((1, tm, REST_W), lambda i, m: (i, m, 0)),
                  pl.BlockSpec((1, HALO, REST_W), lambda i, m: (i, jnp.maximum(m * hpt - 1, 0), 0)),
                  pl.BlockSpec((1, HALO, REST_W),
                               lambda i, m: (i, jnp.minimum((m + 1) * hpt, nhalo - 1), 0)),
                  pl.BlockSpec((1, tm, d), lambda i, m: (i, m, 0)),
                  const((1, RET_W)),
                  const((CONF_KERNEL, SUBLANES, CONF_W)),
                  const((1, CONF_W)), const((1, CONF_W)), const((1, CONF_W)),
                  const((SC_KERNEL, SUBLANES, SC_W)),
                  _resident(w_out, layer),
                  const((1, d)),
                  pl.BlockSpec((1, 1, d), lambda i, m: (i, 0, 0))],
        out_specs=pl.BlockSpec((1, tm, d), lambda i, m: (i, m, 0)),
        out_shape=jax.ShapeDtypeStruct((b, l, d), F32),
        scratch_shapes=[pltpu.VMEM((tm, RET_W + CONF_W + SC_W), BF16),
                        pltpu.VMEM((len(_tap_plan(CONF_KERNEL)[0]), tm + 2 * HALO, CONF_W), F32),
                        pltpu.VMEM((len(_tap_plan(SC_KERNEL)[0]), tm + 2 * HALO, SC_W), F32),
                        pltpu.VMEM((tm, CONF_W), F32),
                        pltpu.VMEM((tm, d), F32)],
        compiler_params=_params("parallel", "parallel"),
        name="token_mix",
    )(y_ret, rest, rest, rest, x, row(ret_norm_g), tile_rows(conf_dw_w), row(conf_dw_b),
      row(conf_ln_g), row(conf_ln_b), tile_rows(sc_dw_w), w_out, row(g_post), gate)


def _mlp_kernel(x_ref, gpre_ref, sh_ref, sc_ref, w1_ref, w2_ref, gpost_ref, gate_ref, o_ref, hb_ref,
                acc_ref, *, tm):
    f = pl.program_id(2)

    @pl.when(f == 0)
    def _prologue():
        _norm_modulate(x_ref, gpre_ref, sh_ref, sc_ref, hb_ref, tm)
        acc_ref[...] = jnp.zeros(acc_ref.shape, F32)

    h1 = jnp.dot(hb_ref[...], w1_ref[...], preferred_element_type=F32)
    a = jnp.maximum(h1, 0.0)
    acc_ref[...] += jnp.dot((a * a).astype(BF16), w2_ref[...], preferred_element_type=F32)

    @pl.when(f == pl.num_programs(2) - 1)
    def _epilogue():
        _gated_residual(x_ref, acc_ref, gpost_ref, gate_ref, o_ref, tm)


def _mlp(x, g_pre, shift, scale, w1, w2, g_post, gate, layer):
    b, l, d = x.shape
    dff = w1.shape[2]
    tm = min(1024, l)
    tf = 512
    row = lambda a: a.reshape(1, -1)
    mod = pl.BlockSpec((1, 1, d), lambda i, m, f: (i, 0, 0))
    return pl.pallas_call(
        functools.partial(_mlp_kernel, tm=tm),
        grid=(b, l // tm, dff // tf),
        in_specs=[pl.BlockSpec((1, tm, d), lambda i, m, f: (i, m, 0)),
                  pl.BlockSpec((1, d), lambda i, m, f: (0, 0)),
                  mod, mod,
                  pl.BlockSpec((None, d, tf), lambda i, m, f: (layer, 0, f)),
                  pl.BlockSpec((None, tf, d), lambda i, m, f: (layer, f, 0)),
                  pl.BlockSpec((1, d), lambda i, m, f: (0, 0)),
                  mod],
        out_specs=pl.BlockSpec((1, tm, d), lambda i, m, f: (i, m, 0)),
        out_shape=jax.ShapeDtypeStruct((b, l, d), F32),
        scratch_shapes=[pltpu.VMEM((tm, d), BF16), pltpu.VMEM((tm, d), F32)],
        compiler_params=_params("parallel", "parallel", "arbitrary"),
        name="mlp",
    )(x, row(g_pre), shift, scale, w1, w2, row(g_post), gate)


def kernel(x, c, ctx, c_ctx, w_ada, b_ada, g_pre_mix, g_post_mix, g_pre_ffn, g_post_ffn, w_in, ret_decay_fwd, ret_decay_bwd, ret_norm_g, conf_dw_w, conf_dw_b, conf_ln_g, conf_ln_b, sc_dw_w, w_out, w_ffn1, w_ffn2):
    depth = w_in.shape[0]
    b, l, d = x.shape
    lc = ctx.shape[1]

    cond = jnp.zeros((8, d), F32).at[:b].set(c).at[b].set(c_ctx)
    mods = _ada_modulation(cond, w_ada, b_ada)
    tab_x = _rope_tables(l, True)
    tab_c = _rope_tables(lc, False)
    zero_state = jnp.zeros((b, HEADS, HEAD_DIM, HEAD_DIM), F32)
    w_in, w_out, w_ffn1, w_ffn2 = (w.astype(BF16) for w in (w_in, w_out, w_ffn1, w_ffn2))

    xc = ctx
    for layer in range(depth):
        last = layer == depth - 1
        mod_x = [t[:, None, :] for t in jnp.split(mods[layer, :b], 6, axis=-1)]
        mod_c = [jnp.broadcast_to(t[:, None, :], (b, 1, d))
                 for t in jnp.split(mods[layer, b:b + 1], 6, axis=-1)]
        lg = jnp.stack([jax.nn.log_sigmoid(ret_decay_fwd[layer].astype(F32)),
                        jax.nn.log_sigmoid(ret_decay_bwd[layer].astype(F32))])
        g_pre = g_pre_mix[layer].reshape(1, d)
        mix_params = (ret_norm_g[layer], conf_dw_w[layer], conf_dw_b[layer], conf_ln_g[layer],
                      conf_ln_b[layer], sc_dw_w[layer], w_out, g_post_mix[layer])
        mlp_params = (w_ffn1, w_ffn2, g_post_ffn[layer])

        qkv_c, rest_c = _inproj(xc, g_pre, mod_c[0], mod_c[1], tab_c, w_in, layer)
        y_c, s_f, s_b = _retention(qkv_c, lg, zero_state, zero_state)

        qkv, rest = _inproj(x, g_pre, mod_x[0], mod_x[1], tab_x, w_in, layer)
        y, _, _ = _retention(qkv, lg, s_f, s_b)
        x = _token_mix(y, rest, x, *mix_params, mod_x[2], layer)
        x = _mlp(x, g_pre_ffn[layer], mod_x[3], mod_x[4], *mlp_params, mod_x[5], layer)

        if not last:
            xc = _token_mix(y_c, rest_c, xc, *mix_params, mod_c[2], layer)
            xc = _mlp(xc.reshape(1, b * lc, d), g_pre_ffn[layer], mod_c[3][:1], mod_c[4][:1],
                      *mlp_params, mod_c[5][:1], layer).reshape(b, lc, d)
    return x
```

```python
import functools

import jax
import jax.numpy as jnp
from jax import lax
from jax.experimental import pallas as pl
from jax.experimental.pallas import tpu as pltpu

F32 = jnp.float32
BF16 = jnp.bfloat16

GRID_W = 64
HEADS = 8
HEAD_DIM = 128
RET_W = HEADS * HEAD_DIM
CONF_W = 512
CONF_KERNEL = 31
SC_W = 512
SC_KERNEL = 3
QKV_W = 3 * RET_W
REST_W = RET_W + 2 * CONF_W + 3 * SC_W
ROPE_BASE = 10000.0
EPS = 1e-6

LANES = 128
SUBLANES = 8
HALO = 16
ROWS = 32
RET_CHUNK = 256
RET_HEADS_PER_STEP = 2
VMEM_LIMIT = 58 * 1024 * 1024


def _rms(x, g):
    return x * lax.rsqrt(jnp.mean(x * x, axis=-1, keepdims=True) + EPS) * g


def _row_loop(tm, body, unroll=4, block=ROWS):
    def step(i, carry):
        body(pl.multiple_of(i * block, block))
        return carry

    lax.fori_loop(0, tm // block, step, 0, unroll=unroll)


def _norm_modulate(x_ref, g_ref, sh_ref, sc_ref, hb_ref, tm, zero_ref=None):
    gain = g_ref[...] * (1.0 + sc_ref[0])
    shift = sh_ref[0]

    half = ROWS // 2

    def rows(r):
        hb_ref[pl.ds(r, half), :] = (_rms(x_ref[0, pl.ds(r, half), :], gain) + shift).astype(BF16)
        if zero_ref is not None:
            zero_ref[pl.ds(r, half), :] = jnp.zeros((half, zero_ref.shape[1]), F32)

    _row_loop(tm, rows, unroll=8, block=half)


def _gated_residual(x_ref, y_ref, gpost_ref, gate_ref, o_ref, tm):
    gain = gate_ref[0] * gpost_ref[...]

    half = ROWS // 2

    def rows(r):
        o_ref[0, pl.ds(r, half), :] = x_ref[0, pl.ds(r, half), :] + _rms(y_ref[pl.ds(r, half), :], gain)

    _row_loop(tm, rows, unroll=8, block=half)


def _sigmoid(x):
    return 1.0 / (1.0 + jnp.exp(-x))


def _params(*sem):
    return pltpu.CompilerParams(dimension_semantics=sem, vmem_limit_bytes=VMEM_LIMIT)


def _resident(w_all, layer):
    return pl.BlockSpec((None,) + w_all.shape[1:], lambda *_: (layer, 0, 0),
                        pipeline_mode=pl.Buffered(1))


def _ada_kernel(cond_ref, w_ref, b_ref, o_ref):
    s = cond_ref[...]
    s = s * _sigmoid(s)
    w = w_ref[0]
    s_hi = s.astype(BF16)
    s_lo = (s - s_hi.astype(F32)).astype(BF16)
    w_hi = w.astype(BF16)
    w_lo = (w - w_hi.astype(F32)).astype(BF16)
    dot = functools.partial(jnp.dot, preferred_element_type=F32)
    o_ref[0] = dot(s_hi, w_hi) + dot(s_lo, w_hi) + dot(s_hi, w_lo) + b_ref[0]


def _ada_modulation(cond, w_ada, b_ada):
    depth, d, n = w_ada.shape
    r = cond.shape[0]
    tn = 1024
    return pl.pallas_call(
        _ada_kernel,
        grid=(depth, n // tn),
        in_specs=[pl.BlockSpec((r, d), lambda l, j: (0, 0)),
                  pl.BlockSpec((1, d, tn), lambda l, j: (l, 0, j)),
                  pl.BlockSpec((1, 1, tn), lambda l, j: (l, 0, j))],
        out_specs=pl.BlockSpec((1, r, tn), lambda l, j: (l, 0, j)),
        out_shape=jax.ShapeDtypeStruct((depth, r, n), F32),
        compiler_params=_params("parallel", "parallel"),
        name="ada_modulation",
    )(cond, w_ada, b_ada.reshape(depth, 1, n))


def _inproj_kernel(x_ref, g_ref, sh_ref, sc_ref, cos_ref, sin_ref, w_ref, qkv_ref, rest_ref, hb_ref,
                   *, tm):
    _norm_modulate(x_ref, g_ref, sh_ref, sc_ref, hb_ref, tm)
    hb = hb_ref[...]

    lane = lax.broadcasted_iota(jnp.int32, (tm, LANES), 1)
    first_of_pair = (lane & 32) == 0
    nt = 512
    for n in range((QKV_W + REST_W) // nt):
        a = jnp.dot(hb, w_ref[:, n * nt:(n + 1) * nt], preferred_element_type=F32)
        if n * nt < QKV_W:
            for j in range(nt // LANES):
                head = (n * nt) // LANES + j
                blk = a[:, j * LANES:(j + 1) * LANES]
                if head < 2 * HEADS:
                    swapped = jnp.where(first_of_pair, pltpu.roll(blk, 96, 1), pltpu.roll(blk, 32, 1))
                    blk = blk * cos_ref[...] + swapped * sin_ref[...]
                if HEADS <= head < 2 * HEADS:
                    blk = blk * HEAD_DIM ** -0.5
                qkv_ref[0, head] = blk.astype(BF16)
        else:
            c0 = n * nt - QKV_W
            rest_ref[0, :, c0:c0 + nt] = a.astype(BF16)


def _inproj(x, g_pre, shift, scale, tables, w_in, layer):
    b, l, d = x.shape
    tm = min(512, l)
    return pl.pallas_call(
        functools.partial(_inproj_kernel, tm=tm),
        grid=(b, l // tm),
        in_specs=[pl.BlockSpec((1, tm, d), lambda i, m: (i, m, 0)),
                  pl.BlockSpec((1, d), lambda i, m: (0, 0)),
                  pl.BlockSpec((1, 1, d), lambda i, m: (i, 0, 0)),
                  pl.BlockSpec((1, 1, d), lambda i, m: (i, 0, 0)),
                  pl.BlockSpec((tm, LANES), lambda i, m: (m, 0)),
                  pl.BlockSpec((tm, LANES), lambda i, m: (m, 0)),
                  _resident(w_in, layer)],
        out_specs=[pl.BlockSpec((1, 3 * HEADS, tm, HEAD_DIM), lambda i, m: (i, 0, m, 0)),
                   pl.BlockSpec((1, tm, REST_W), lambda i, m: (i, m, 0))],
        out_shape=[jax.ShapeDtypeStruct((b, 3 * HEADS, l, HEAD_DIM), BF16),
                   jax.ShapeDtypeStruct((b, l, REST_W), BF16)],
        scratch_shapes=[pltpu.VMEM((tm, d), BF16)],
        compiler_params=_params("parallel", "parallel"),
        name="in_projection",
    )(x, g_pre, shift, scale, *tables, w_in)


def _rope_tables(l, rotate):
    if not rotate:
        return jnp.ones((l, HEAD_DIM), F32), jnp.zeros((l, HEAD_DIM), F32)
    half = HEAD_DIM // 4
    rows_n = l // GRID_W
    lane = jnp.arange(HEAD_DIM)
    freqs = (ROPE_BASE ** (-jnp.arange(half, dtype=F32) / half))[lane % half]
    sign = jnp.where(lane % (2 * half) < half, -1.0, 1.0).astype(F32)
    ang_r = jnp.arange(rows_n, dtype=F32)[:, None] * freqs
    ang_c = jnp.arange(GRID_W, dtype=F32)[:, None] * freqs
    cr, sr, cc, sn = lax.optimization_barrier(
        (jnp.cos(ang_r), jnp.sin(ang_r) * sign, jnp.cos(ang_c), jnp.sin(ang_c) * sign))
    by_row = lane < 2 * half

    def spread(rows, cols):
        return jnp.where(by_row, rows[:, None, :], cols[None, :, :]).reshape(l, HEAD_DIM)

    return spread(cr, cc), spread(sr, sn)


def _retention_kernel(lg_ref, q_ref, k_ref, v_ref, f0_ref, g0_ref, y_ref, ffin_ref, gfin_ref,
                      f_ref, g_ref, gs_ref, mask_ref, dec_ref, bd_ref, *, hb, ns, cps):
    c = RET_CHUNK
    h0 = pl.program_id(1) * hb
    s = pl.program_id(2)

    @pl.when(s == 0)
    def _init():
        ii = lax.broadcasted_iota(jnp.int32, (c, c), 0)
        jj = lax.broadcasted_iota(jnp.int32, (c, c), 1)
        diff = (ii - jj).astype(F32)
        pos = lax.broadcasted_iota(jnp.int32, (c, HEAD_DIM), 0).astype(F32)
        for hh in range(hb):
            lgf = lg_ref[0, h0 + hh]
            lgb = lg_ref[1, h0 + hh]
            mask_ref[hh] = (jnp.where(diff >= 0, jnp.exp(lgf * jnp.maximum(diff, 0.0)), 0.0)
                            + jnp.where(diff <= 0, jnp.exp(lgb * jnp.maximum(-diff, 0.0)), 0.0))
            dec_ref[hh, 0] = jnp.exp(lgf * (pos + 1.0))
            dec_ref[hh, 1] = jnp.exp(lgb * (c - pos))
            dec_ref[hh, 2] = jnp.exp(lgf * (c - 1.0 - pos))
            dec_ref[hh, 3] = jnp.exp(lgb * pos)
            bd_ref[hh, 0] = jnp.exp(jnp.full((HEAD_DIM, HEAD_DIM), lgf * c, F32))
            bd_ref[hh, 1] = jnp.exp(jnp.full((HEAD_DIM, HEAD_DIM), lgb * c, F32))
            f_ref[hh] = f0_ref[0, hh]
            g_ref[hh] = g0_ref[0, hh]

    tn_dims = (((0,), (0,)), ((), ()))
    nt_dims = (((1,), (1,)), ((), ()))

    @pl.when(s < ns)
    def _backward():
        def body(t, carry):
            j = cps - 1 - t
            n = (ns - 1 - s) * cps + j
            r = pl.multiple_of(j * c, c)
            for hh in range(hb):
                gs_ref[n, hh] = g_ref[hh].astype(BF16)
                k = k_ref[0, hh, pl.ds(r, c), :]
                v = v_ref[0, hh, pl.ds(r, c), :]
                kb = (k.astype(F32) * dec_ref[hh, 3]).astype(BF16)
                upd = lax.dot_general(kb, v, tn_dims, preferred_element_type=F32)
                g_ref[hh] = g_ref[hh] * bd_ref[hh, 1] + upd
            return carry

        lax.fori_loop(0, cps, body, 0, unroll=min(8, cps))

        @pl.when(s == ns - 1)
        def _():
            for hh in range(hb):
                gfin_ref[0, hh] = g_ref[hh]

    @pl.when(s >= ns)
    def _forward():
        def body(j, carry):
            n = (s - ns) * cps + j
            r = pl.multiple_of(j * c, c)
            for hh in range(hb):
                q = q_ref[0, hh, pl.ds(r, c), :]
                k = k_ref[0, hh, pl.ds(r, c), :]
                v = v_ref[0, hh, pl.ds(r, c), :]
                q32 = q.astype(F32)
                scores = lax.dot_general(q, k, nt_dims, preferred_element_type=F32)
                p = (scores * mask_ref[hh]).astype(BF16)
                qd = jnp.concatenate([(q32 * dec_ref[hh, 0]).astype(BF16),
                                      (q32 * dec_ref[hh, 1]).astype(BF16)], axis=1)
                st = jnp.concatenate([f_ref[hh].astype(BF16), gs_ref[n, hh]], axis=0)
                o = (jnp.dot(p, v, preferred_element_type=F32)
                     + jnp.dot(qd, st, preferred_element_type=F32))
                mu = jnp.mean(o, axis=-1, keepdims=True)
                oc = o - mu
                var = jnp.mean(oc * oc, axis=-1, keepdims=True)
                y_ref[0, hh, pl.ds(r, c), :] = (oc * lax.rsqrt(var + EPS)).astype(BF16)
                kf = (k.astype(F32) * dec_ref[hh, 2]).astype(BF16)
                upd = lax.dot_general(kf, v, tn_dims, preferred_element_type=F32)
                f_ref[hh] = f_ref[hh] * bd_ref[hh, 0] + upd
            return carry

        lax.fori_loop(0, cps, body, 0, unroll=min(8, cps))

        @pl.when(s == 2 * ns - 1)
        def _():
            for hh in range(hb):
                ffin_ref[0, hh] = f_ref[hh]


def _retention(qkv, lg, f0, g0):
    b, _, l, d = qkv.shape
    c = RET_CHUNK
    hb = RET_HEADS_PER_STEP
    sc = min(4096, l)
    ns = l // sc
    cps = sc // c

    def kv_block(s):
        return jnp.where(s < ns, ns - 1 - s, s - ns)

    def q_block(s):
        return jnp.maximum(s - ns, 0)

    hblocks = HEADS // hb
    state_spec = pl.BlockSpec((1, hb, d, d), lambda i, h, s: (i, h, 0, 0))
    return pl.pallas_call(
        functools.partial(_retention_kernel, hb=hb, ns=ns, cps=cps),
        grid=(b, hblocks, 2 * ns),
        in_specs=[pl.BlockSpec(memory_space=pltpu.SMEM),
                  pl.BlockSpec((1, hb, sc, d), lambda i, h, s: (i, h, q_block(s), 0)),
                  pl.BlockSpec((1, hb, sc, d), lambda i, h, s: (i, hblocks + h, kv_block(s), 0)),
                  pl.BlockSpec((1, hb, sc, d), lambda i, h, s: (i, 2 * hblocks + h, kv_block(s), 0)),
                  state_spec, state_spec],
        out_specs=[pl.BlockSpec((1, hb, sc, d), lambda i, h, s: (i, h, q_block(s), 0)),
                   state_spec, state_spec],
        out_shape=[jax.ShapeDtypeStruct((b, HEADS, l, d), BF16),
                   jax.ShapeDtypeStruct((b, HEADS, d, d), F32),
                   jax.ShapeDtypeStruct((b, HEADS, d, d), F32)],
        scratch_shapes=[pltpu.VMEM((hb, d, d), F32),
                        pltpu.VMEM((hb, d, d), F32),
                        pltpu.VMEM((l // c, hb, d, d), BF16),
                        pltpu.VMEM((hb, c, c), F32),
                        pltpu.VMEM((hb, 4, c, d), F32),
                        pltpu.VMEM((hb, 2, d, d), F32)],
        compiler_params=_params("parallel", "parallel", "arbitrary"),
        name="retention",
    )(lg, qkv, qkv, qkv, f0, g0)


def _tap_plan(kernel_size):
    offsets = [HALO - kernel_size // 2 + j for j in range(kernel_size)]
    phases = sorted({o % SUBLANES for o in offsets})
    return phases, [(phases.index(o % SUBLANES), o - o % SUBLANES) for o in offsets]


def _shifted_copies(ph_ref, phases, tm):
    n = tm + 2 * HALO - SUBLANES
    for slot, p in enumerate(phases):
        if p:
            ph_ref[slot, 0:n, :] = ph_ref[0, p:p + n, :]


def _mix_kernel(y_ref, rest_ref, prev_ref, next_ref, x_ref, gret_ref, cw_ref, cb_ref, lng_ref,
                lnb_ref, sw_ref, wout_ref, gpost_ref, gate_ref, o_ref,
                mixed_ref, uph_ref, zph_ref, conv_ref, proj_ref, *, tm):
    m = pl.program_id(1)
    last = pl.num_programs(1) - 1
    c_a, c_b = RET_W, RET_W + CONF_W
    c_x, c_bb, c_c = RET_W + 2 * CONF_W, RET_W + 2 * CONF_W + SC_W, RET_W + 2 * CONF_W + 2 * SC_W
    conf_phases, conf_taps = _tap_plan(CONF_KERNEL)
    sc_phases, sc_taps = _tap_plan(SC_KERNEL)

    def glu(blk):
        a = blk[:, c_a:c_a + CONF_W].astype(F32)
        bgate = blk[:, c_b:c_b + CONF_W].astype(F32)
        return a * _sigmoid(bgate)

    def sc_in(blk):
        return blk[:, c_c:c_c + SC_W].astype(F32) * blk[:, c_x:c_x + SC_W].astype(F32)

    pv = prev_ref[0]
    nx = next_ref[0]
    uph_ref[0, 0:HALO, :] = jnp.where(m > 0, glu(pv), 0.0)
    zph_ref[0, 0:HALO, :] = jnp.where(m > 0, sc_in(pv), 0.0)
    uph_ref[0, HALO + tm:2 * HALO + tm, :] = jnp.where(m < last, glu(nx), 0.0)
    zph_ref[0, HALO + tm:2 * HALO + tm, :] = jnp.where(m < last, sc_in(nx), 0.0)

    gret = gret_ref[...]

    def gate_rows(r):
        blk = rest_ref[0, pl.ds(r, ROWS), :]
        gt = blk[:, 0:RET_W].astype(F32)
        y = jnp.concatenate([y_ref[0, h, pl.ds(r, ROWS), :] for h in range(HEADS)], axis=1)
        ret = y.astype(F32) * gret * (gt * _sigmoid(gt))
        mixed_ref[pl.ds(r, ROWS), 0:RET_W] = ret.astype(BF16)
        uph_ref[0, pl.ds(HALO + r, ROWS), :] = glu(blk)
        zph_ref[0, pl.ds(HALO + r, ROWS), :] = sc_in(blk)

    _row_loop(tm, gate_rows)
    _shifted_copies(uph_ref, conf_phases, tm)
    _shifted_copies(zph_ref, sc_phases, tm)

    cbias = cb_ref[...]
    lng = lng_ref[...]
    lnb = lnb_ref[...]

    def taps(ph_ref, w_ref, plan, r, init):
        groups = [init] * (ROWS // SUBLANES)
        for j, (slot, base) in enumerate(plan):
            w = w_ref[j]
            for i in range(len(groups)):
                rows = pl.ds(pl.multiple_of(r + base + i * SUBLANES, SUBLANES), SUBLANES)
                groups[i] = groups[i] + ph_ref[slot, rows, :] * w
        return jnp.concatenate(groups, axis=0)

    def conv_rows(r):
        conv_ref[pl.ds(r, ROWS), :] = taps(uph_ref, cw_ref, conf_taps, r,
                                           jnp.broadcast_to(cbias, (SUBLANES, CONF_W)))

    _row_loop(tm, conv_rows, unroll=1)

    def norm_rows(r):
        acc = conv_ref[pl.ds(r, ROWS), :]
        mu = jnp.mean(acc, axis=-1, keepdims=True)
        ac = acc - mu
        var = jnp.mean(ac * ac, axis=-1, keepdims=True)
        u = ac * lax.rsqrt(var + EPS) * lng + lnb
        mixed_ref[pl.ds(r, ROWS), RET_W:RET_W + CONF_W] = (u * _sigmoid(u)).astype(BF16)
        zacc = taps(zph_ref, sw_ref, sc_taps, r, jnp.zeros((SUBLANES, SC_W), F32))
        scb = rest_ref[0, pl.ds(r, ROWS), c_bb:c_bb + SC_W].astype(F32)
        mixed_ref[pl.ds(r, ROWS), RET_W + CONF_W:RET_W + CONF_W + SC_W] = (scb * zacc).astype(BF16)

    _row_loop(tm, norm_rows)

    proj_ref[...] = jnp.dot(mixed_ref[...], wout_ref[...], preferred_element_type=F32)
    _gated_residual(x_ref, proj_ref, gpost_ref, gate_ref, o_ref, tm)


def _token_mix(y_ret, rest, x, ret_norm_g, conf_dw_w, conf_dw_b, conf_ln_g, conf_ln_b, sc_dw_w,
               w_out, g_post, gate, layer):
    b, l, d = x.shape
    tm = min(512, l)
    hpt = tm // HALO
    nhalo = l // HALO
    row = lambda a: a.reshape(1, -1)
    tile_rows = lambda w: jnp.broadcast_to(w[:, None, :], (w.shape[0], SUBLANES, w.shape[1]))
    const = lambda shape: pl.BlockSpec(shape, lambda i, m: (0,) * len(shape))
    return pl.pallas_call(
        functools.partial(_mix_kernel, tm=tm),
        grid=(b, l // tm),
        in_specs=[pl.BlockSpec((1, HEADS, tm, HEAD_DIM), lambda i, m: (i, 0, m, 0)),
                  pl.BlockSpec((1, tm, REST_W), lambda i, m: (i, m, 0)),
                  pl.BlockSpec((1, HALO, REST_W), lambda i, m: (i, jnp.maximum(m * hpt - 1, 0), 0)),
                  pl.BlockSpec((1, HALO, REST_W),
                               lambda i, m: (i, jnp.minimum((m + 1) * hpt, nhalo - 1), 0)),
                  pl.BlockSpec((1, tm, d), lambda i, m: (i, m, 0)),
                  const((1, RET_W)),
                  const((CONF_KERNEL, SUBLANES, CONF_W)),
                  const((1, CONF_W)), const((1, CONF_W)), const((1, CONF_W)),
                  const((SC_KERNEL, SUBLANES, SC_W)),
                  _resident(w_out, layer),
                  const((1, d)),
                  pl.BlockSpec((1, 1, d), lambda i, m: (i, 0, 0))],
        out_specs=pl.BlockSpec((1, tm, d), lambda i, m: (i, m, 0)),
        out_shape=jax.ShapeDtypeStruct((b, l, d), F32),
        scratch_shapes=[pltpu.VMEM((tm, RET_W + CONF_W + SC_W), BF16),
                        pltpu.VMEM((len(_tap_plan(CONF_KERNEL)[0]), tm + 2 * HALO, CONF_W), F32),
                        pltpu.VMEM((len(_tap_plan(SC_KERNEL)[0]), tm + 2 * HALO, SC_W), F32),
                        pltpu.VMEM((tm, CONF_W), F32),
                        pltpu.VMEM((tm, d), F32)],
        compiler_params=_params("parallel", "parallel"),
        name="token_mix",
    )(y_ret, rest, rest, rest, x, row(ret_norm_g), tile_rows(conf_dw_w), row(conf_dw_b),
      row(conf_ln_g), row(conf_ln_b), tile_rows(sc_dw_w), w_out, row(g_post), gate)


def _mlp_kernel(x_ref, gpre_ref, sh_ref, sc_ref, w1_ref, w2_ref, gpost_ref, gate_ref, o_ref, hb_ref,
                acc_ref, *, tm):
    f = pl.program_id(2)

    @pl.when(f == 0)
    def _prologue():
        _norm_modulate(x_ref, gpre_ref, sh_ref, sc_ref, hb_ref, tm, zero_ref=acc_ref)

    h1 = jnp.dot(hb_ref[...], w1_ref[...], preferred_element_type=F32)
    a = jnp.maximum(h1, 0.0)
    acc_ref[...] += jnp.dot((a * a).astype(BF16), w2_ref[...], preferred_element_type=F32)

    @pl.when(f == pl.num_programs(2) - 1)
    def _epilogue():
        _gated_residual(x_ref, acc_ref, gpost_ref, gate_ref, o_ref, tm)


def _mlp(x, g_pre, shift, scale, w1, w2, g_post, gate, layer):
    b, l, d = x.shape
    dff = w1.shape[2]
    tm = min(1024, l)
    tf = 512
    row = lambda a: a.reshape(1, -1)
    mod = pl.BlockSpec((1, 1, d), lambda i, m, f: (i, 0, 0))
    return pl.pallas_call(
        functools.partial(_mlp_kernel, tm=tm),
        grid=(b, l // tm, dff // tf),
        in_specs=[pl.BlockSpec((1, tm, d), lambda i, m, f: (i, m, 0)),
                  pl.BlockSpec((1, d), lambda i, m, f: (0, 0)),
                  mod, mod,
                  pl.BlockSpec((None, d, tf), lambda i, m, f: (layer, 0, f)),
                  pl.BlockSpec((None, tf, d), lambda i, m, f: (layer, f, 0)),
                  pl.BlockSpec((1, d), lambda i, m, f: (0, 0)),
                  mod],
        out_specs=pl.BlockSpec((1, tm, d), lambda i, m, f: (i, m, 0)),
        out_shape=jax.ShapeDtypeStruct((b, l, d), F32),
        scratch_shapes=[pltpu.VMEM((tm, d), BF16), pltpu.VMEM((tm, d), F32)],
        compiler_params=_params("parallel", "parallel", "arbitrary"),
        name="mlp",
    )(x, row(g_pre), shift, scale, w1, w2, row(g_post), gate)


def kernel(x, c, ctx, c_ctx, w_ada, b_ada, g_pre_mix, g_post_mix, g_pre_ffn, g_post_ffn, w_in, ret_decay_fwd, ret_decay_bwd, ret_norm_g, conf_dw_w, conf_dw_b, conf_ln_g, conf_ln_b, sc_dw_w, w_out, w_ffn1, w_ffn2):
    depth = w_in.shape[0]
    b, l, d = x.shape
    lc = ctx.shape[1]

    cond = jnp.zeros((8, d), F32).at[:b].set(c).at[b].set(c_ctx)
    mods = _ada_modulation(cond, w_ada, b_ada)
    tab_x = _rope_tables(l, True)
    tab_c = _rope_tables(lc, False)
    zero_state = jnp.zeros((b, HEADS, HEAD_DIM, HEAD_DIM), F32)
    w_in, w_out, w_ffn1, w_ffn2 = (w.astype(BF16) for w in (w_in, w_out, w_ffn1, w_ffn2))

    xc = ctx
    for layer in range(depth):
        last = layer == depth - 1
        mod_x = [t[:, None, :] for t in jnp.split(mods[layer, :b], 6, axis=-1)]
        mod_c = [jnp.broadcast_to(t[:, None, :], (b, 1, d))
                 for t in jnp.split(mods[layer, b:b + 1], 6, axis=-1)]
        lg = jnp.stack([jax.nn.log_sigmoid(ret_decay_fwd[layer].astype(F32)),
                        jax.nn.log_sigmoid(ret_decay_bwd[layer].astype(F32))])
        g_pre = g_pre_mix[layer].reshape(1, d)
        mix_params = (ret_norm_g[layer], conf_dw_w[layer], conf_dw_b[layer], conf_ln_g[layer],
                      conf_ln_b[layer], sc_dw_w[layer], w_out, g_post_mix[layer])
        mlp_params = (w_ffn1, w_ffn2, g_post_ffn[layer])

        qkv_c, rest_c = _inproj(xc, g_pre, mod_c[0], mod_c[1], tab_c, w_in, layer)
        y_c, s_f, s_b = _retention(qkv_c, lg, zero_state, zero_state)

        qkv, rest = _inproj(x, g_pre, mod_x[0], mod_x[1], tab_x, w_in, layer)
        y, _, _ = _retention(qkv, lg, s_f, s_b)
        x = _token_mix(y, rest, x, *mix_params, mod_x[2], layer)
        x = _mlp(x, g_pre_ffn[layer], mod_x[3], mod_x[4], *mlp_params, mod_x[5], layer)

        if not last:
            xc = _token_mix(y_c, rest_c, xc, *mix_params, mod_c[2], layer)
            xc = _mlp(xc.reshape(1, b * lc, d), g_pre_ffn[layer], mod_c[3][:1], mod_c[4][:1],
                      *mlp_params, mod_c[5][:1], layer).reshape(b, lc, d)
    return x
```

```python
import functools

import jax
import jax.numpy as jnp
from jax import lax
from jax.experimental import pallas as pl
from jax.experimental.pallas import tpu as pltpu

F32 = jnp.float32
BF16 = jnp.bfloat16

GRID_W = 64
HEADS = 8
HEAD_DIM = 128
RET_W = HEADS * HEAD_DIM
CONF_W = 512
CONF_KERNEL = 31
SC_W = 512
SC_KERNEL = 3
QKV_W = 3 * RET_W
REST_W = RET_W + 2 * CONF_W + 3 * SC_W
ROPE_BASE = 10000.0
EPS = 1e-6

LANES = 128
SUBLANES = 8
HALO = 16
ROWS = 32
RET_CHUNK = 256
RET_HEADS_PER_STEP = 2
VMEM_LIMIT = 58 * 1024 * 1024


def _rms(x, g):
    return x * lax.rsqrt(jnp.mean(x * x, axis=-1, keepdims=True) + EPS) * g


def _row_loop(tm, body, unroll=4, block=ROWS):
    def step(i, carry):
        body(pl.multiple_of(i * block, block))
        return carry

    lax.fori_loop(0, tm // block, step, 0, unroll=unroll)


def _norm_modulate(x_ref, g_ref, sh_ref, sc_ref, hb_ref, tm, zero_ref=None):
    gain = g_ref[...] * (1.0 + sc_ref[0])
    shift = sh_ref[0]

    half = ROWS // 2

    def rows(r):
        hb_ref[pl.ds(r, half), :] = (_rms(x_ref[0, pl.ds(r, half), :], gain) + shift).astype(BF16)
        if zero_ref is not None:
            zero_ref[pl.ds(r, half), :] = jnp.zeros((half, zero_ref.shape[1]), F32)

    _row_loop(tm, rows, unroll=8, block=half)


def _gated_residual(x_ref, y_ref, gpost_ref, gate_ref, o_ref, tm):
    gain = gate_ref[0] * gpost_ref[...]

    half = ROWS // 2

    def rows(r):
        o_ref[0, pl.ds(r, half), :] = x_ref[0, pl.ds(r, half), :] + _rms(y_ref[pl.ds(r, half), :], gain)

    _row_loop(tm, rows, unroll=8, block=half)


def _sigmoid(x):
    return 1.0 / (1.0 + jnp.exp(-x))


def _params(*sem):
    return pltpu.CompilerParams(dimension_semantics=sem, vmem_limit_bytes=VMEM_LIMIT)


def _resident(w_all, layer):
    return pl.BlockSpec((None,) + w_all.shape[1:], lambda *_: (layer, 0, 0),
                        pipeline_mode=pl.Buffered(1))


def _ada_kernel(cond_ref, w_ref, b_ref, o_ref):
    s = cond_ref[...]
    s = s * _sigmoid(s)
    w = w_ref[0]
    s_hi = s.astype(BF16)
    s_lo = (s - s_hi.astype(F32)).astype(BF16)
    w_hi = w.astype(BF16)
    w_lo = (w - w_hi.astype(F32)).astype(BF16)
    dot = functools.partial(jnp.dot, preferred_element_type=F32)
    o_ref[0] = dot(s_hi, w_hi) + dot(s_lo, w_hi) + dot(s_hi, w_lo) + b_ref[0]


def _ada_modulation(cond, w_ada, b_ada):
    depth, d, n = w_ada.shape
    r = cond.shape[0]
    tn = 1024
    return pl.pallas_call(
        _ada_kernel,
        grid=(depth, n // tn),
        in_specs=[pl.BlockSpec((r, d), lambda l, j: (0, 0)),
                  pl.BlockSpec((1, d, tn), lambda l, j: (l, 0, j)),
                  pl.BlockSpec((1, 1, tn), lambda l, j: (l, 0, j))],
        out_specs=pl.BlockSpec((1, r, tn), lambda l, j: (l, 0, j)),
        out_shape=jax.ShapeDtypeStruct((depth, r, n), F32),
        compiler_params=_params("parallel", "parallel"),
        name="ada_modulation",
    )(cond, w_ada, b_ada.reshape(depth, 1, n))


def _inproj_kernel(x_ref, g_ref, sh_ref, sc_ref, cos_ref, sin_ref, w_ref, qkv_ref, rest_ref, hb_ref,
                   *, tm):
    _norm_modulate(x_ref, g_ref, sh_ref, sc_ref, hb_ref, tm)
    hb = hb_ref[...]

    lane = lax.broadcasted_iota(jnp.int32, (tm, LANES), 1)
    first_of_pair = (lane & 32) == 0
    nt = 512
    for n in range((QKV_W + REST_W) // nt):
        a = jnp.dot(hb, w_ref[:, n * nt:(n + 1) * nt], preferred_element_type=F32)
        if n * nt < QKV_W:
            for j in range(nt // LANES):
                head = (n * nt) // LANES + j
                blk = a[:, j * LANES:(j + 1) * LANES]
                if head < 2 * HEADS:
                    swapped = jnp.where(first_of_pair, pltpu.roll(blk, 96, 1), pltpu.roll(blk, 32, 1))
                    blk = blk * cos_ref[...] + swapped * sin_ref[...]
                if HEADS <= head < 2 * HEADS:
                    blk = blk * HEAD_DIM ** -0.5
                qkv_ref[0, head] = blk.astype(BF16)
        else:
            c0 = n * nt - QKV_W
            rest_ref[0, :, c0:c0 + nt] = a.astype(BF16)


def _inproj(x, g_pre, shift, scale, tables, w_in, layer):
    b, l, d = x.shape
    tm = min(512, l)
    return pl.pallas_call(
        functools.partial(_inproj_kernel, tm=tm),
        grid=(b, l // tm),
        in_specs=[pl.BlockSpec((1, tm, d), lambda i, m: (i, m, 0)),
                  pl.BlockSpec((1, d), lambda i, m: (0, 0)),
                  pl.BlockSpec((1, 1, d), lambda i, m: (i, 0, 0)),
                  pl.BlockSpec((1, 1, d), lambda i, m: (i, 0, 0)),
                  pl.BlockSpec((tm, LANES), lambda i, m: (m, 0)),
                  pl.BlockSpec((tm, LANES), lambda i, m: (m, 0)),
                  _resident(w_in, layer)],
        out_specs=[pl.BlockSpec((1, 3 * HEADS, tm, HEAD_DIM), lambda i, m: (i, 0, m, 0)),
                   pl.BlockSpec((1, tm, REST_W), lambda i, m: (i, m, 0))],
        out_shape=[jax.ShapeDtypeStruct((b, 3 * HEADS, l, HEAD_DIM), BF16),
                   jax.ShapeDtypeStruct((b, l, REST_W), BF16)],
        scratch_shapes=[pltpu.VMEM((tm, d), BF16)],
        compiler_params=_params("parallel", "parallel"),
        name="in_projection",
    )(x, g_pre, shift, scale, *tables, w_in)


def _rope_tables(l, rotate):
    if not rotate:
        return jnp.ones((l, HEAD_DIM), F32), jnp.zeros((l, HEAD_DIM), F32)
    half = HEAD_DIM // 4
    rows_n = l // GRID_W
    lane = jnp.arange(HEAD_DIM)
    freqs = (ROPE_BASE ** (-jnp.arange(half, dtype=F32) / half))[lane % half]
    sign = jnp.where(lane % (2 * half) < half, -1.0, 1.0).astype(F32)
    ang_r = jnp.arange(rows_n, dtype=F32)[:, None] * freqs
    ang_c = jnp.arange(GRID_W, dtype=F32)[:, None] * freqs
    cr, sr, cc, sn = lax.optimization_barrier(
        (jnp.cos(ang_r), jnp.sin(ang_r) * sign, jnp.cos(ang_c), jnp.sin(ang_c) * sign))
    by_row = lane < 2 * half

    def spread(rows, cols):
        return jnp.where(by_row, rows[:, None, :], cols[None, :, :]).reshape(l, HEAD_DIM)

    return spread(cr, cc), spread(sr, sn)


def _retention_kernel(lg_ref, q_ref, k_ref, v_ref, f0_ref, g0_ref, y_ref, ffin_ref, gfin_ref,
                      f_ref, g_ref, gs_ref, mask_ref, dec_ref, bd_ref, *, hb, ns, cps):
    c = RET_CHUNK
    h0 = pl.program_id(1) * hb
    s = pl.program_id(2)

    @pl.when(s == 0)
    def _init():
        ii = lax.broadcasted_iota(jnp.int32, (c, c), 0)
        jj = lax.broadcasted_iota(jnp.int32, (c, c), 1)
        diff = (ii - jj).astype(F32)
        pos = lax.broadcasted_iota(jnp.int32, (c, HEAD_DIM), 0).astype(F32)
        for hh in range(hb):
            lgf = lg_ref[0, h0 + hh]
            lgb = lg_ref[1, h0 + hh]
            mask_ref[hh] = (jnp.where(diff >= 0, jnp.exp(lgf * jnp.maximum(diff, 0.0)), 0.0)
                            + jnp.where(diff <= 0, jnp.exp(lgb * jnp.maximum(-diff, 0.0)), 0.0))
            dec_ref[hh, 0] = jnp.exp(lgf * (pos + 1.0))
            dec_ref[hh, 1] = jnp.exp(lgb * (c - pos))
            dec_ref[hh, 2] = jnp.exp(lgf * (c - 1.0 - pos))
            dec_ref[hh, 3] = jnp.exp(lgb * pos)
            bd_ref[hh, 0] = jnp.exp(jnp.full((HEAD_DIM, HEAD_DIM), lgf * c, F32))
            bd_ref[hh, 1] = jnp.exp(jnp.full((HEAD_DIM, HEAD_DIM), lgb * c, F32))
            f_ref[hh] = f0_ref[0, hh]
            g_ref[hh] = g0_ref[0, hh]

    tn_dims = (((0,), (0,)), ((), ()))
    nt_dims = (((1,), (1,)), ((), ()))

    @pl.when(s < ns)
    def _backward():
        def body(t, carry):
            j = cps - 1 - t
            n = (ns - 1 - s) * cps + j
            r = pl.multiple_of(j * c, c)
            for hh in range(hb):
                gs_ref[n, hh] = g_ref[hh].astype(BF16)
                k = k_ref[0, hh, pl.ds(r, c), :]
                v = v_ref[0, hh, pl.ds(r, c), :]
                kb = (k.astype(F32) * dec_ref[hh, 3]).astype(BF16)
                upd = lax.dot_general(kb, v, tn_dims, preferred_element_type=F32)
                g_ref[hh] = g_ref[hh] * bd_ref[hh, 1] + upd
            return carry

        lax.fori_loop(0, cps, body, 0, unroll=min(8, cps))

        @pl.when(s == ns - 1)
        def _():
            for hh in range(hb):
                gfin_ref[0, hh] = g_ref[hh]

    @pl.when(s >= ns)
    def _forward():
        def body(j, carry):
            n = (s - ns) * cps + j
            r = pl.multiple_of(j * c, c)
            for hh in range(hb):
                q = q_ref[0, hh, pl.ds(r, c), :]
                k = k_ref[0, hh, pl.ds(r, c), :]
                v = v_ref[0, hh, pl.ds(r, c), :]
                q32 = q.astype(F32)
                scores = lax.dot_general(q, k, nt_dims, preferred_element_type=F32)
                p = (scores * mask_ref[hh]).astype(BF16)
                qd = jnp.concatenate([(q32 * dec_ref[hh, 0]).astype(BF16),
                                      (q32 * dec_ref[hh, 1]).astype(BF16)], axis=1)
                st = jnp.concatenate([f_ref[hh].astype(BF16), gs_ref[n, hh]], axis=0)
                o = (jnp.dot(p, v, preferred_element_type=F32)
                     + jnp.dot(qd, st, preferred_element_type=F32))
                mu = jnp.mean(o, axis=-1, keepdims=True)
                oc = o - mu
                var = jnp.mean(oc * oc, axis=-1, keepdims=True)
                y_ref[0, hh, pl.ds(r, c), :] = (oc * lax.rsqrt(var + EPS)).astype(BF16)
                kf = (k.astype(F32) * dec_ref[hh, 2]).astype(BF16)
                upd = lax.dot_general(kf, v, tn_dims, preferred_element_type=F32)
                f_ref[hh] = f_ref[hh] * bd_ref[hh, 0] + upd
            return carry

        lax.fori_loop(0, cps, body, 0, unroll=min(8, cps))

        @pl.when(s == 2 * ns - 1)
        def _():
            for hh in range(hb):
                ffin_ref[0, hh] = f_ref[hh]


def _retention(qkv, lg, f0, g0):
    b, _, l, d = qkv.shape
    c = RET_CHUNK
    hb = RET_HEADS_PER_STEP
    sc = min(4096, l)
    ns = l // sc
    cps = sc // c

    def kv_block(s):
        return jnp.where(s < ns, ns - 1 - s, s - ns)

    def q_block(s):
        return jnp.maximum(s - ns, 0)

    hblocks = HEADS // hb
    state_spec = pl.BlockSpec((1, hb, d, d), lambda i, h, s: (i, h, 0, 0))
    return pl.pallas_call(
        functools.partial(_retention_kernel, hb=hb, ns=ns, cps=cps),
        grid=(b, hblocks, 2 * ns),
        in_specs=[pl.BlockSpec(memory_space=pltpu.SMEM),
                  pl.BlockSpec((1, hb, sc, d), lambda i, h, s: (i, h, q_block(s), 0)),
                  pl.BlockSpec((1, hb, sc, d), lambda i, h, s: (i, hblocks + h, kv_block(s), 0)),
                  pl.BlockSpec((1, hb, sc, d), lambda i, h, s: (i, 2 * hblocks + h, kv_block(s), 0)),
                  state_spec, state_spec],
        out_specs=[pl.BlockSpec((1, hb, sc, d), lambda i, h, s: (i, h, q_block(s), 0)),
                   state_spec, state_spec],
        out_shape=[jax.ShapeDtypeStruct((b, HEADS, l, d), BF16),
                   jax.ShapeDtypeStruct((b, HEADS, d, d), F32),
                   jax.ShapeDtypeStruct((b, HEADS, d, d), F32)],
        scratch_shapes=[pltpu.VMEM((hb, d, d), F32),
                        pltpu.VMEM((hb, d, d), F32),
                        pltpu.VMEM((l // c, hb, d, d), BF16),
                        pltpu.VMEM((hb, c, c), F32),
                        pltpu.VMEM((hb, 4, c, d), F32),
                        pltpu.VMEM((hb, 2, d, d), F32)],
        compiler_params=_params("parallel", "parallel", "arbitrary"),
        name="retention",
    )(lg, qkv, qkv, qkv, f0, g0)


def _tap_plan(kernel_size):
    offsets = [HALO - kernel_size // 2 + j for j in range(kernel_size)]
    phases = sorted({o % SUBLANES for o in offsets})
    return phases, [(phases.index(o % SUBLANES), o - o % SUBLANES) for o in offsets]


def _shifted_copies(ph_ref, phases, tm):
    n = tm + 2 * HALO - SUBLANES
    for slot, p in enumerate(phases):
        if p:
            ph_ref[slot, 0:n, :] = ph_ref[0, p:p + n, :]


def _mix_kernel(y_ref, rest_ref, prev_ref, next_ref, x_ref, gret_ref, cw_ref, cb_ref, lng_ref,
                lnb_ref, sw_ref, wout_ref, gpost_ref, gate_ref, o_ref,
                mixed_ref, uph_ref, zph_ref, conv_ref, proj_ref, *, tm):
    m = pl.program_id(1)
    last = pl.num_programs(1) - 1
    c_a, c_b = RET_W, RET_W + CONF_W
    c_x, c_bb, c_c = RET_W + 2 * CONF_W, RET_W + 2 * CONF_W + SC_W, RET_W + 2 * CONF_W + 2 * SC_W
    conf_phases, conf_taps = _tap_plan(CONF_KERNEL)
    sc_phases, sc_taps = _tap_plan(SC_KERNEL)

    def glu(blk):
        a = blk[:, c_a:c_a + CONF_W].astype(F32)
        bgate = blk[:, c_b:c_b + CONF_W].astype(F32)
        return a * _sigmoid(bgate)

    def sc_in(blk):
        return blk[:, c_c:c_c + SC_W].astype(F32) * blk[:, c_x:c_x + SC_W].astype(F32)

    pv = prev_ref[0]
    nx = next_ref[0]
    uph_ref[0, 0:HALO, :] = jnp.where(m > 0, glu(pv), 0.0)
    zph_ref[0, 0:HALO, :] = jnp.where(m > 0, sc_in(pv), 0.0)
    uph_ref[0, HALO + tm:2 * HALO + tm, :] = jnp.where(m < last, glu(nx), 0.0)
    zph_ref[0, HALO + tm:2 * HALO + tm, :] = jnp.where(m < last, sc_in(nx), 0.0)

    gret = gret_ref[...]

    def gate_rows(r):
        blk = rest_ref[0, pl.ds(r, ROWS), :]
        gt = blk[:, 0:RET_W].astype(F32)
        y = jnp.concatenate([y_ref[0, h, pl.ds(r, ROWS), :] for h in range(HEADS)], axis=1)
        ret = y.astype(F32) * gret * (gt * _sigmoid(gt))
        mixed_ref[pl.ds(r, ROWS), 0:RET_W] = ret.astype(BF16)
        uph_ref[0, pl.ds(HALO + r, ROWS), :] = glu(blk)
        zph_ref[0, pl.ds(HALO + r, ROWS), :] = sc_in(blk)

    _row_loop(tm, gate_rows, unroll=8)
    _shifted_copies(uph_ref, conf_phases, tm)
    _shifted_copies(zph_ref, sc_phases, tm)

    cbias = cb_ref[...]
    lng = lng_ref[...]
    lnb = lnb_ref[...]

    def taps(ph_ref, w_ref, plan, r, init):
        groups = [init] * (ROWS // SUBLANES)
        for j, (slot, base) in enumerate(plan):
            w = w_ref[j]
            for i in range(len(groups)):
                rows = pl.ds(pl.multiple_of(r + base + i * SUBLANES, SUBLANES), SUBLANES)
                groups[i] = groups[i] + ph_ref[slot, rows, :] * w
        return jnp.concatenate(groups, axis=0)

    def conv_rows(r):
        conv_ref[pl.ds(r, ROWS), :] = taps(uph_ref, cw_ref, conf_taps, r,
                                           jnp.broadcast_to(cbias, (SUBLANES, CONF_W)))

    _row_loop(tm, conv_rows, unroll=2)

    def norm_rows(r):
        acc = conv_ref[pl.ds(r, ROWS), :]
        mu = jnp.mean(acc, axis=-1, keepdims=True)
        ac = acc - mu
        var = jnp.mean(ac * ac, axis=-1, keepdims=True)
        u = ac * lax.rsqrt(var + EPS) * lng + lnb
        mixed_ref[pl.ds(r, ROWS), RET_W:RET_W + CONF_W] = (u * _sigmoid(u)).astype(BF16)
        zacc = taps(zph_ref, sw_ref, sc_taps, r, jnp.zeros((SUBLANES, SC_W), F32))
        scb = rest_ref[0, pl.ds(r, ROWS), c_bb:c_bb + SC_W].astype(F32)
        mixed_ref[pl.ds(r, ROWS), RET_W + CONF_W:RET_W + CONF_W + SC_W] = (scb * zacc).astype(BF16)

    _row_loop(tm, norm_rows, unroll=8)

    proj_ref[...] = jnp.dot(mixed_ref[...], wout_ref[...], preferred_element_type=F32)
    _gated_residual(x_ref, proj_ref, gpost_ref, gate_ref, o_ref, tm)


def _token_mix(y_ret, rest, x, ret_norm_g, conf_dw_w, conf_dw_b, conf_ln_g, conf_ln_b, sc_dw_w,
               w_out, g_post, gate, layer):
    b, l, d = x.shape
    tm = min(512, l)
    hpt = tm // HALO
    nhalo = l // HALO
    row = lambda a: a.reshape(1, -1)
    tile_rows = lambda w: jnp.broadcast_to(w[:, None, :], (w.shape[0], SUBLANES, w.shape[1]))
    const = lambda shape: pl.BlockSpec(shape, lambda i, m: (0,) * len(shape))
    return pl.pallas_call(
        functools.partial(_mix_kernel, tm=tm),
        grid=(b, l // tm),
        in_specs=[pl.BlockSpec((1, HEADS, tm, HEAD_DIM), lambda i, m: (i, 0, m, 0)),
                  pl.BlockSpec((1, tm, REST_W), lambda i, m: (i, m, 0)),
                  pl.BlockSpec((1, HALO, REST_W), lambda i, m: (i, jnp.maximum(m * hpt - 1, 0), 0)),
                  pl.BlockSpec((1, HALO, REST_W),
                               lambda i, m: (i, jnp.minimum((m + 1) * hpt, nhalo - 1), 0)),
                  pl.BlockSpec((1, tm, d), lambda i, m: (i, m, 0)),
                  const((1, RET_W)),
                  const((CONF_KERNEL, SUBLANES, CONF_W)),
                  const((1, CONF_W)), const((1, CONF_W)), const((1, CONF_W)),
                  const((SC_KERNEL, SUBLANES, SC_W)),
                  _resident(w_out, layer),
                  const((1, d)),
                  pl.BlockSpec((1, 1, d), lambda i, m: (i, 0, 0))],
        out_specs=pl.BlockSpec((1, tm, d), lambda i, m: (i, m, 0)),
        out_shape=jax.ShapeDtypeStruct((b, l, d), F32),
        scratch_shapes=[pltpu.VMEM((tm, RET_W + CONF_W + SC_W), BF16),
                        pltpu.VMEM((len(_tap_plan(CONF_KERNEL)[0]), tm + 2 * HALO, CONF_W), F32),
                        pltpu.VMEM((len(_tap_plan(SC_KERNEL)[0]), tm + 2 * HALO, SC_W), F32),
                        pltpu.VMEM((tm, CONF_W), F32),
                        pltpu.VMEM((tm, d), F32)],
        compiler_params=_params("parallel", "parallel"),
        name="token_mix",
    )(y_ret, rest, rest, rest, x, row(ret_norm_g), tile_rows(conf_dw_w), row(conf_dw_b),
      row(conf_ln_g), row(conf_ln_b), tile_rows(sc_dw_w), w_out, row(g_post), gate)


def _mlp_kernel(x_ref, gpre_ref, sh_ref, sc_ref, w1_ref, w2_ref, gpost_ref, gate_ref, o_ref, hb_ref,
                acc_ref, *, tm):
    f = pl.program_id(2)

    @pl.when(f == 0)
    def _prologue():
        _norm_modulate(x_ref, gpre_ref, sh_ref, sc_ref, hb_ref, tm, zero_ref=acc_ref)

    h1 = jnp.dot(hb_ref[...], w1_ref[...], preferred_element_type=F32)
    a = jnp.maximum(h1, 0.0)
    acc_ref[...] += jnp.dot((a * a).astype(BF16), w2_ref[...], preferred_element_type=F32)

    @pl.when(f == pl.num_programs(2) - 1)
    def _epilogue():
        _gated_residual(x_ref, acc_ref, gpost_ref, gate_ref, o_ref, tm)


def _mlp(x, g_pre, shift, scale, w1, w2, g_post, gate, layer):
    b, l, d = x.shape
    dff = w1.shape[2]
    tm = min(1024, l)
    tf = 512
    row = lambda a: a.reshape(1, -1)
    mod = pl.BlockSpec((1, 1, d), lambda i, m, f: (i, 0, 0))
    return pl.pallas_call(
        functools.partial(_mlp_kernel, tm=tm),
        grid=(b, l // tm, dff // tf),
        in_specs=[pl.BlockSpec((1, tm, d), lambda i, m, f: (i, m, 0)),
                  pl.BlockSpec((1, d), lambda i, m, f: (0, 0)),
                  mod, mod,
                  pl.BlockSpec((None, d, tf), lambda i, m, f: (layer, 0, f)),
                  pl.BlockSpec((None, tf, d), lambda i, m, f: (layer, f, 0)),
                  pl.BlockSpec((1, d), lambda i, m, f: (0, 0)),
                  mod],
        out_specs=pl.BlockSpec((1, tm, d), lambda i, m, f: (i, m, 0)),
        out_shape=jax.ShapeDtypeStruct((b, l, d), F32),
        scratch_shapes=[pltpu.VMEM((tm, d), BF16), pltpu.VMEM((tm, d), F32)],
        compiler_params=_params("parallel", "parallel", "arbitrary"),
        name="mlp",
    )(x, row(g_pre), shift, scale, w1, w2, row(g_post), gate)


def kernel(x, c, ctx, c_ctx, w_ada, b_ada, g_pre_mix, g_post_mix, g_pre_ffn, g_post_ffn, w_in, ret_decay_fwd, ret_decay_bwd, ret_norm_g, conf_dw_w, conf_dw_b, conf_ln_g, conf_ln_b, sc_dw_w, w_out, w_ffn1, w_ffn2):
    depth = w_in.shape[0]
    b, l, d = x.shape
    lc = ctx.shape[1]

    cond = jnp.zeros((8, d), F32).at[:b].set(c).at[b].set(c_ctx)
    mods = _ada_modulation(cond, w_ada, b_ada)
    tab_x = _rope_tables(l, True)
    tab_c = _rope_tables(lc, False)
    zero_state = jnp.zeros((b, HEADS, HEAD_DIM, HEAD_DIM), F32)
    w_in, w_out, w_ffn1, w_ffn2 = (w.astype(BF16) for w in (w_in, w_out, w_ffn1, w_ffn2))

    xc = ctx
    for layer in range(depth):
        last = layer == depth - 1
        mod_x = [t[:, None, :] for t in jnp.split(mods[layer, :b], 6, axis=-1)]
        mod_c = [jnp.broadcast_to(t[:, None, :], (b, 1, d))
                 for t in jnp.split(mods[layer, b:b + 1], 6, axis=-1)]
        lg = jnp.stack([jax.nn.log_sigmoid(ret_decay_fwd[layer].astype(F32)),
                        jax.nn.log_sigmoid(ret_decay_bwd[layer].astype(F32))])
        g_pre = g_pre_mix[layer].reshape(1, d)
        mix_params = (ret_norm_g[layer], conf_dw_w[layer], conf_dw_b[layer], conf_ln_g[layer],
                      conf_ln_b[layer], sc_dw_w[layer], w_out, g_post_mix[layer])
        mlp_params = (w_ffn1, w_ffn2, g_post_ffn[layer])

        qkv_c, rest_c = _inproj(xc, g_pre, mod_c[0], mod_c[1], tab_c, w_in, layer)
        y_c, s_f, s_b = _retention(qkv_c, lg, zero_state, zero_state)

        qkv, rest = _inproj(x, g_pre, mod_x[0], mod_x[1], tab_x, w_in, layer)
        y, _, _ = _retention(qkv, lg, s_f, s_b)
        x = _token_mix(y, rest, x, *mix_params, mod_x[2], layer)
        x = _mlp(x, g_pre_ffn[layer], mod_x[3], mod_x[4], *mlp_params, mod_x[5], layer)

        if not last:
            xc = _token_mix(y_c, rest_c, xc, *mix_params, mod_c[2], layer)
            xc = _mlp(xc.reshape(1, b * lc, d), g_pre_ffn[layer], mod_c[3][:1], mod_c[4][:1],
                      *mlp_params, mod_c[5][:1], layer).reshape(b, lc, d)
    return x
```
